```python
import math, functools
import jax, jax.numpy as jnp
from jax import lax
import numpy as np

D_MODEL = 1024
BATCH = 8
SEQ = 2048
DEPTH = 4
DEC_BATCH = 128
DEC_SEQ = 8
PAST_LEN = 2048
PAGE_SIZE = 128

N_A_LAYERS = DEPTH // 2
N_B_LAYERS = DEPTH - N_A_LAYERS
HGRN_EXPAND = 128
HGRN_HEADS = D_MODEL // HGRN_EXPAND
HGRN_DK = HGRN_EXPAND
HGRN_DV = D_MODEL // HGRN_HEADS
HGRN_FD = HGRN_HEADS * HGRN_DK
HGRN_CHUNK = 64
FOX_HEAD_DIM = 64
FOX_HEADS = D_MODEL // FOX_HEAD_DIM
FOX_SCALE = FOX_HEAD_DIM ** -0.5
Q_BLOCK = 128
D_FF = ((8 * D_MODEL // 3 + 127) // 128) * 128
CONV_W = 3
EPS = 1e-6

kernel_name = "yoco_hgrn2_fox_convffn_step"


def rmsnorm(x, g):
    xf = x.astype(jnp.float32)
    r = lax.rsqrt(jnp.mean(xf * xf, axis=-1, keepdims=True) + EPS)
    return (xf * r).astype(x.dtype) * g


def hgrn2_recurrence(q, k, logf, v, s0):
    B, T, H, DK = q.shape
    C = math.gcd(T, HGRN_CHUNK)
    n = T // C
    f32 = jnp.float32

    def to_chunks(a):
        return a.astype(f32).reshape(B, n, C, H, a.shape[-1]).transpose(1, 0, 3, 2, 4)

    tri = jnp.tril(jnp.ones((C, C), dtype=bool))[:, :, None]

    def step(S, xs):
        qc, kc, lc, vc = xs
        b = jnp.cumsum(lc, axis=2)
        diff = b[:, :, :, None, :] - b[:, :, None, :, :]
        decay = jnp.where(tri, jnp.exp(jnp.where(tri, diff, 0.0)), 0.0)
        A = jnp.einsum('bhtd,bhsd,bhtsd->bhts', qc, kc, decay)
        o = (jnp.einsum('bhts,bhsv->bhtv', A, vc)
             + jnp.einsum('bhtd,bhdv->bhtv', qc * jnp.exp(b), S))
        b_last = b[:, :, -1:, :]
        S = (jnp.exp(b_last[:, :, 0, :])[..., None] * S
             + jnp.einsum('bhsd,bhsv->bhdv', kc * jnp.exp(b_last - b), vc))
        return S, o

    S, o = lax.scan(step, s0.astype(f32), (to_chunks(q), to_chunks(k), to_chunks(logf), to_chunks(v)))
    o = o.transpose(1, 0, 3, 2, 4).reshape(B, T, H, -1)
    return o, S


def hgrn2_mixer(h, s0, w_in, lb, g_onorm, w_o, first_layer):
    B, T, _ = h.shape
    proj = h @ w_in
    q, fr, i, g = jnp.split(proj, [HGRN_FD, 2 * HGRN_FD, 2 * HGRN_FD + D_MODEL], axis=-1)
    q = jax.nn.silu(q)
    fr = fr.astype(jnp.float32)
    if first_layer:
        logf = jax.nn.log_sigmoid(fr)
        k = jax.nn.sigmoid(-fr)
    else:
        lb = lb.astype(jnp.float32)
        f = lb + (1.0 - lb) * jax.nn.sigmoid(fr)
        logf = jnp.log(f)
        k = (1.0 - lb) * jax.nn.sigmoid(-fr)
    heads = lambda a, d: a.reshape(B, T, HGRN_HEADS, d)
    o, S = hgrn2_recurrence(heads(q, HGRN_DK), heads(k, HGRN_DK), heads(logf, HGRN_DK),
                            heads(i, HGRN_DV), s0)
    o = rmsnorm(o.astype(h.dtype), g_onorm).reshape(B, T, D_MODEL) * jax.nn.silu(g)
    return o @ w_o, S


def conv_ffn(h, conv_prev, w_in, conv_w, conv_b, w_out):
    T = h.shape[1]
    u, gate = jnp.split(h @ w_in, 2, axis=-1)
    ext = jnp.concatenate([conv_prev.astype(u.dtype), u], axis=1)
    a = conv_b + sum(ext[:, j:j + T] * conv_w[j] for j in range(CONV_W))
    out = (jax.nn.silu(a) * gate) @ w_out
    return out, ext[:, -(CONV_W - 1):]


def shared_kv(h, kv_norm, w_kvf, b_f, k_norm):
    B, T, _ = h.shape
    hn = rmsnorm(h, kv_norm)
    k, v, fl = jnp.split(hn @ w_kvf, [D_MODEL, 2 * D_MODEL], axis=-1)
    k = rmsnorm(k.reshape(B, T, FOX_HEADS, FOX_HEAD_DIM), k_norm)
    v = v.reshape(B, T, FOX_HEADS, FOX_HEAD_DIM)
    logf = jax.nn.log_sigmoid((fl + b_f).astype(jnp.float32))
    return k, v, logf


def fox_block_attention(q, k, v, cum):
    B, S, H, D = q.shape
    nb = S // Q_BLOCK
    qb = q.reshape(B, nb, Q_BLOCK, H, D).transpose(1, 0, 2, 3, 4)
    cqb = cum.reshape(B, nb, Q_BLOCK, H).transpose(1, 0, 3, 2)
    qpos = jnp.arange(S).reshape(nb, Q_BLOCK)
    kpos = jnp.arange(S)
    cum_k = cum.transpose(0, 2, 1)

    def one_block(args):
        qi, cqi, pi = args
        s = jnp.einsum('bqhd,bkhd->bhqk', qi, k).astype(jnp.float32) * FOX_SCALE
        logits = s + (cqi[..., :, None] - cum_k[..., None, :])
        mask = kpos[None, :] <= pi[:, None]
        p = jax.nn.softmax(jnp.where(mask, logits, -jnp.inf), axis=-1)
        return jnp.einsum('bhqk,bkhd->bqhd', p.astype(v.dtype), v)

    out = lax.map(one_block, (qb, cqb, qpos))
    return out.transpose(1, 0, 2, 3, 4).reshape(B, S, H, D)


def fox_decode_attention(q, k_new, v_new, c_new, k_past, v_past, c_past):
    T = q.shape[1]
    P = k_past.shape[1]
    cq = c_new.transpose(0, 2, 1)
    s_past = (jnp.einsum('bqhd,bkhd->bhqk', q, k_past).astype(jnp.float32) * FOX_SCALE
              + (cq[..., :, None] - c_past.transpose(0, 2, 1)[..., None, :]))
    s_new = (jnp.einsum('bqhd,bkhd->bhqk', q, k_new).astype(jnp.float32) * FOX_SCALE
             + (cq[..., :, None] - cq[..., None, :]))
    tri = jnp.tril(jnp.ones((T, T), dtype=bool))
    s_new = jnp.where(tri, s_new, -jnp.inf)
    p = jax.nn.softmax(jnp.concatenate([s_past, s_new], axis=-1), axis=-1)
    return (jnp.einsum('bhqk,bkhd->bqhd', p[..., :P].astype(v_past.dtype), v_past)
            + jnp.einsum('bhqk,bkhd->bqhd', p[..., P:].astype(v_new.dtype), v_new))


def run_trunk(x, hgrn_s0, conv_s0, past, norm_mix, norm_ffn, hgrn_w_in, hgrn_lb_param, hgrn_onorm,
              hgrn_w_o, kv_norm, w_kvf, fox_b_f, k_norm, fox_w_qg, q_norm, fox_w_o,
              ffn_w_in, ffn_conv_w, ffn_conv_b, ffn_w_out):
    B, T, _ = x.shape
    p = jax.nn.softmax(hgrn_lb_param.astype(jnp.float32), axis=0)
    lower_bounds = jnp.cumsum(p, axis=0) - p[0]
    h = x
    new_hgrn, new_conv = [], []
    k = v = logf = c_new = c_past = k_past = v_past = None
    for layer in range(DEPTH):
        hn = rmsnorm(h, norm_mix[layer])
        if layer < N_A_LAYERS:
            mix, S = hgrn2_mixer(hn, hgrn_s0[layer], hgrn_w_in[layer], lower_bounds[layer],
                                 hgrn_onorm[layer], hgrn_w_o[layer], first_layer=(layer == 0))
            new_hgrn.append(S)
        else:
            j = layer - N_A_LAYERS
            qg = hn @ fox_w_qg[j]
            q, gate = jnp.split(qg, 2, axis=-1)
            q = rmsnorm(q.reshape(B, T, FOX_HEADS, FOX_HEAD_DIM), q_norm[j])
            if past is None:
                o = fox_block_attention(q, k, v, c_new)
            else:
                o = fox_decode_attention(q, k, v, c_new, k_past, v_past, c_past)
            mix = (o.reshape(B, T, D_MODEL) * jax.nn.sigmoid(gate)) @ fox_w_o[j]
        h = h + mix
        f, cs = conv_ffn(rmsnorm(h, norm_ffn[layer]), conv_s0[layer], ffn_w_in[layer],
                         ffn_conv_w[layer], ffn_conv_b[layer], ffn_w_out[layer])
        new_conv.append(cs)
        h = h + f
        if layer == N_A_LAYERS - 1:
            k, v, logf = shared_kv(h, kv_norm, w_kvf, fox_b_f, k_norm)
            if past is None:
                c_new = jnp.cumsum(logf, axis=1)
            else:
                k_past, v_past, logf_past = past
                P = logf_past.shape[1]
                c_all = jnp.cumsum(jnp.concatenate([logf_past.astype(jnp.float32), logf], axis=1), axis=1)
                c_past, c_new = c_all[:, :P], c_all[:, P:]
    return h, k, v, logf, jnp.stack(new_hgrn), jnp.stack(new_conv)


def setup_inputs(seed: int = 0) -> dict:
    key = jax.random.key(seed)
    ks = jax.random.split(key, 32)
    f32 = jnp.float32
    n_pages = PAST_LEN // PAGE_SIZE
    n_used = DEC_BATCH * n_pages
    n_phys = n_used + max(1, n_used // 4)

    def nrm(k, shape, s=1.0):
        return s * jax.random.normal(k, shape, f32)

    def gain(k, shape):
        return 1.0 + 0.05 * jax.random.normal(k, shape, f32)

    d_inv = D_MODEL ** -0.5
    x_prompt = nrm(ks[0], (BATCH, SEQ, D_MODEL))
    x_sample = nrm(ks[1], (DEC_BATCH, DEC_SEQ, D_MODEL))
    cache_k = nrm(ks[2], (n_phys, PAGE_SIZE, FOX_HEADS, FOX_HEAD_DIM))
    cache_v = nrm(ks[3], (n_phys, PAGE_SIZE, FOX_HEADS, FOX_HEAD_DIM))
    cache_logf = jax.nn.log_sigmoid(jax.random.uniform(ks[4], (n_phys, PAGE_SIZE, FOX_HEADS), f32, 2.0, 6.0))
    state_hgrn = nrm(ks[5], (N_A_LAYERS, DEC_BATCH, HGRN_HEADS, HGRN_DK, HGRN_DV), 0.5)
    state_conv = nrm(ks[6], (DEPTH, DEC_BATCH, CONV_W - 1, D_FF))
    page_table = jax.random.permutation(ks[7], n_phys)[:n_used].reshape(DEC_BATCH, n_pages).astype(jnp.int32)

    norm_mix = gain(ks[8], (DEPTH, D_MODEL))
    norm_ffn = gain(ks[9], (DEPTH, D_MODEL))
    hgrn_w_in = nrm(ks[10], (N_A_LAYERS, D_MODEL, 2 * HGRN_FD + 2 * D_MODEL), d_inv)
    hgrn_lb_param = nrm(ks[11], (N_A_LAYERS, HGRN_FD))
    hgrn_onorm = gain(ks[12], (N_A_LAYERS, HGRN_DV))
    hgrn_w_o = nrm(ks[13], (N_A_LAYERS, D_MODEL, D_MODEL), d_inv)
    kv_norm = gain(ks[14], (D_MODEL,))
    w_kvf = jnp.concatenate([nrm(ks[15], (D_MODEL, 2 * D_MODEL), d_inv),
                             nrm(ks[16], (D_MODEL, FOX_HEADS), 0.3 * d_inv)], axis=1)
    fox_b_f = jax.random.uniform(ks[17], (FOX_HEADS,), f32, 2.0, 6.0)
    k_norm = gain(ks[18], (FOX_HEAD_DIM,))
    fox_w_qg = nrm(ks[19], (N_B_LAYERS, D_MODEL, 2 * D_MODEL), d_inv)
    q_norm = gain(ks[20], (N_B_LAYERS, FOX_HEAD_DIM))
    fox_w_o = nrm(ks[21], (N_B_LAYERS, D_MODEL, D_MODEL), d_inv)
    ffn_w_in = nrm(ks[22], (DEPTH, D_MODEL, 2 * D_FF), d_inv)
    ffn_conv_w = nrm(ks[23], (DEPTH, CONV_W, D_FF), CONV_W ** -0.5)
    ffn_conv_b = nrm(ks[24], (DEPTH, D_FF), 0.02)
    ffn_w_out = nrm(ks[25], (DEPTH, D_FF, D_MODEL), D_FF ** -0.5)
    return {"x_prompt": x_prompt, "x_sample": x_sample, "cache_k": cache_k, "cache_v": cache_v,
            "cache_logf": cache_logf, "state_hgrn": state_hgrn, "state_conv": state_conv,
            "page_table": page_table, "norm_mix": norm_mix, "norm_ffn": norm_ffn,
            "hgrn_w_in": hgrn_w_in, "hgrn_lb_param": hgrn_lb_param, "hgrn_onorm": hgrn_onorm,
            "hgrn_w_o": hgrn_w_o, "kv_norm": kv_norm, "w_kvf": w_kvf, "fox_b_f": fox_b_f,
            "k_norm": k_norm, "fox_w_qg": fox_w_qg, "q_norm": q_norm, "fox_w_o": fox_w_o,
            "ffn_w_in": ffn_w_in, "ffn_conv_w": ffn_conv_w, "ffn_conv_b": ffn_conv_b,
            "ffn_w_out": ffn_w_out}


def reference(x_prompt, x_sample, cache_k, cache_v, cache_logf, state_hgrn, state_conv, page_table,
              norm_mix, norm_ffn, hgrn_w_in, hgrn_lb_param, hgrn_onorm, hgrn_w_o, kv_norm, w_kvf,
              fox_b_f, k_norm, fox_w_qg, q_norm, fox_w_o, ffn_w_in, ffn_conv_w, ffn_conv_b, ffn_w_out):
    trunk = functools.partial(
        run_trunk, norm_mix=norm_mix, norm_ffn=norm_ffn, hgrn_w_in=hgrn_w_in,
        hgrn_lb_param=hgrn_lb_param, hgrn_onorm=hgrn_onorm, hgrn_w_o=hgrn_w_o, kv_norm=kv_norm,
        w_kvf=w_kvf, fox_b_f=fox_b_f, k_norm=k_norm, fox_w_qg=fox_w_qg, q_norm=q_norm,
        fox_w_o=fox_w_o, ffn_w_in=ffn_w_in, ffn_conv_w=ffn_conv_w, ffn_conv_b=ffn_conv_b,
        ffn_w_out=ffn_w_out)

    B = x_prompt.shape[0]
    hgrn0 = jnp.zeros((N_A_LAYERS, B, HGRN_HEADS, HGRN_DK, HGRN_DV), jnp.float32)
    conv0 = jnp.zeros((DEPTH, B, CONV_W - 1, D_FF), x_prompt.dtype)
    y_prompt, k_prompt, v_prompt, logf_prompt, hgrn_prompt, conv_prompt = trunk(
        x_prompt, hgrn0, conv0, None)

    Bd, n_pages = page_table.shape
    page = cache_k.shape[1]
    k_past = cache_k[page_table].reshape(Bd, n_pages * page, FOX_HEADS, FOX_HEAD_DIM)
    v_past = cache_v[page_table].reshape(Bd, n_pages * page, FOX_HEADS, FOX_HEAD_DIM)
    logf_past = cache_logf[page_table].reshape(Bd, n_pages * page, FOX_HEADS)
    y_sample, k_sample, v_sample, logf_sample, hgrn_sample, conv_sample = trunk(
        x_sample, state_hgrn, state_conv, (k_past, v_past, logf_past))

    return (y_prompt, y_sample, k_prompt, v_prompt, logf_prompt, k_sample, v_sample, logf_sample,
            hgrn_prompt, hgrn_sample, conv_prompt, conv_sample)
```

```python
import functools

import numpy as np
import jax
import jax.numpy as jnp
from jax import lax
from jax.experimental import pallas as pl
from jax.experimental.pallas import tpu as pltpu

F32 = jnp.float32
BF16 = jnp.bfloat16
EPS = 1e-6

LANES = 128
SUBLANES = 8
VMEM_LIMIT_BYTES = 56 * 1024 * 1024

HGRN_HEAD = 128
FOX_HEAD = 64
CONV_W = 3
CHUNK = 128
ROW_TILE = 256
ATTN_TILE = 256
PAGE = 128


def _params(semantics):
    return pltpu.CompilerParams(dimension_semantics=semantics, vmem_limit_bytes=VMEM_LIMIT_BYTES)


def _dot(a, b):
    return jnp.dot(a, b, preferred_element_type=F32)


def _dot_nt(a, b):
    return lax.dot_general(a, b, (((1,), (1,)), ((), ())), preferred_element_type=F32)


def _split3(x):
    hi = x.astype(BF16)
    r = x - hi.astype(F32)
    mid = r.astype(BF16)
    lo = (r - mid.astype(F32)).astype(BF16)
    return hi, mid, lo


def _dot3_right(t, x):
    hi, mid, lo = _split3(x)
    return _dot(t, hi) + _dot(t, mid) + _dot(t, lo)


def _dot3_left(x, t):
    hi, mid, lo = _split3(x)
    return _dot(hi, t) + _dot(mid, t) + _dot(lo, t)


def _rms(x, g):
    r = lax.rsqrt(jnp.mean(x * x, axis=-1, keepdims=True) + EPS)
    return x * r * g


def _sigmoid(x):
    return 1.0 / (1.0 + jnp.exp(-x))


def _log_sigmoid(x):
    return jnp.minimum(x, 0.0) - jnp.log(1.0 + jnp.exp(-jnp.abs(x)))


def _group_rms(x, bd_ref, width):
    ss = _dot((x * x).astype(BF16), bd_ref[...])
    return x * lax.rsqrt(ss * (1.0 / width) + EPS)


def _level_mats(rows, seg):
    t = np.arange(rows)[:, None]
    j = np.arange(rows)[None, :]
    same = (t // seg) == (j // seg)
    mats = [same & (j <= t), same & (j > t)]
    lv = np.full((rows, rows), -1, np.int32)
    lv[np.arange(rows), np.arange(rows)] = 0
    h, level = seg // 2, 1
    while h >= 1:
        blk = t // (2 * h)
        m = blk * 2 * h + h
        in_blk = (j // (2 * h)) == blk
        upper = t >= m
        mats.append(in_blk & np.where(upper, (j >= m) & (j <= t), (j >= t + 1) & (j <= m - 1)))
        lv[in_blk & upper & ((j % (2 * h)) < h)] = level
        h //= 2
        level += 1
    tall = np.concatenate([m.astype(np.float32) for m in mats], axis=0)
    return jnp.asarray(tall, BF16), jnp.asarray(lv), level - 1


def _block_diag_ones(n, width):
    i = np.arange(n)
    return jnp.asarray((i[:, None] // width) == (i[None, :] // width), BF16)


def _seg_lower_tri(n, seg):
    i = np.arange(n)
    return jnp.asarray(((i[:, None] // seg) == (i[None, :] // seg)) & (i[None, :] <= i[:, None]), BF16)


def _upper_tri(n):
    i = np.arange(n)
    return jnp.asarray(i[:, None] <= i[None, :], BF16)


def _head_mask(heads, width):
    lane = np.arange(heads * width)
    return jnp.asarray((lane[None, :] // width) == np.arange(heads)[:, None], F32)


def _hgrn_in_kernel(x_ref, g_ref, w_ref, lbp_ref, q_ref, k_ref, lf_ref, i_ref, gt_ref, *, layer):
    d = q_ref.shape[1]
    hn = _rms(x_ref[...], g_ref[...]).astype(BF16)
    proj = _dot(hn, w_ref[...])
    q = proj[:, :d]
    fr = proj[:, d:2 * d]
    q_ref[...] = q * _sigmoid(q)
    i_ref[...] = proj[:, 2 * d:3 * d]
    gt_ref[...] = proj[:, 3 * d:]
    if layer == 0:
        lf_ref[...] = _log_sigmoid(fr)
        k_ref[...] = _sigmoid(-fr)
    else:
        lbp = lbp_ref[...]
        e = jnp.exp(lbp - jnp.max(lbp, axis=0, keepdims=True))
        p = e / jnp.sum(e, axis=0, keepdims=True)
        lb = jnp.sum(p[1:layer + 1], axis=0, keepdims=True)
        lf_ref[...] = jnp.log(lb + (1.0 - lb) * _sigmoid(fr))
        k_ref[...] = (1.0 - lb) * _sigmoid(-fr)


def _hgrn_in(x, g, w, lbp, layer):
    n, d = x.shape
    tm = min(ROW_TILE, n)
    row = pl.BlockSpec((tm, d), lambda i: (i, 0))
    full = lambda a: pl.BlockSpec(a.shape, lambda i: (0,) * a.ndim)
    out = jax.ShapeDtypeStruct((n, d), F32)
    return pl.pallas_call(
        functools.partial(_hgrn_in_kernel, layer=layer),
        grid=(n // tm,),
        in_specs=[row, full(g), full(w), full(lbp)],
        out_specs=[row] * 5,
        out_shape=[out] * 5,
        compiler_params=_params(("arbitrary",)),
    )(x, g, w, lbp)


def _hgrn_intra(q, k, lf, v_bf, tall_ref, lv, n_levels):
    c = q.shape[0]
    hi, mid, lo = _split3(lf)
    d3 = _dot(tall_ref[...], jnp.concatenate([hi, mid, lo], axis=1))
    d = d3[:, :c] + d3[:, c:2 * c] + d3[:, 2 * c:]
    a = jnp.where(lv == 0, _dot_nt(q.astype(BF16), k.astype(BF16)), 0.0)
    for l in range(n_levels):
        e = jnp.exp(d[(2 + l) * c:(3 + l) * c])
        a = jnp.where(lv == l + 1, _dot_nt((q * e).astype(BF16), (k * e).astype(BF16)), a)
    return _dot(a.astype(BF16), v_bf), d[:c], d[c:2 * c]


def _hgrn_prompt_kernel(q_ref, k_ref, lf_ref, v_ref, tall_ref, lv_ref, o_ref, sout_ref, s_scr, *, n_levels):
    j = pl.program_id(2)

    @pl.when(j == 0)
    def _():
        s_scr[...] = jnp.zeros_like(s_scr)

    lv = lv_ref[...]
    for c in range(q_ref.shape[0] // CHUNK):
        rows = pl.ds(c * CHUNK, CHUNK)
        q, k, lf, v = q_ref[rows, :], k_ref[rows, :], lf_ref[rows, :], v_ref[rows, :]
        v_bf = v.astype(BF16)
        o_intra, b, rev = _hgrn_intra(q, k, lf, v_bf, tall_ref, lv, n_levels)
        eb = jnp.exp(b)
        s = s_scr[...]
        o_ref[rows, :] = o_intra + _dot((q * eb).astype(BF16), s.astype(BF16))
        ks_t = (k * jnp.exp(rev)).T.astype(BF16)
        decay = eb.T[:, CHUNK - 1:CHUNK]
        s_scr[...] = s * decay + _dot(ks_t, v_bf)

    @pl.when(j == pl.num_programs(2) - 1)
    def _():
        sout_ref[0, 0] = s_scr[...]


def _hgrn_prompt(q, k, lf, v, batch, consts):
    tall, lv, n_levels = consts
    n, d = q.shape
    heads = d // HGRN_HEAD
    t = n // batch
    tb = min(4 * CHUNK, t)
    nj = t // tb
    blk = pl.BlockSpec((tb, HGRN_HEAD), lambda b, h, j: (b * nj + j, h))
    full = lambda a: pl.BlockSpec(a.shape, lambda b, h, j: (0,) * a.ndim)
    return pl.pallas_call(
        functools.partial(_hgrn_prompt_kernel, n_levels=n_levels),
        grid=(batch, heads, nj),
        in_specs=[blk, blk, blk, blk, full(tall), full(lv)],
        out_specs=[blk, pl.BlockSpec((1, 1, HGRN_HEAD, HGRN_HEAD), lambda b, h, j: (b, h, 0, 0))],
        out_shape=[jax.ShapeDtypeStruct((n, d), F32),
                   jax.ShapeDtypeStruct((batch, heads, HGRN_HEAD, HGRN_HEAD), F32)],
        scratch_shapes=[pltpu.VMEM((HGRN_HEAD, HGRN_HEAD), F32)],
        compiler_params=_params(("arbitrary", "arbitrary", "arbitrary")),
    )(q, k, lf, v, tall, lv)


def _hgrn_decode_kernel(q_ref, k_ref, lf_ref, v_ref, s0_ref, tall_ref, lv_ref, o_ref, sout_ref, *, n_levels, seq):
    q, k, lf, v = q_ref[...], k_ref[...], lf_ref[...], v_ref[...]
    v_bf = v.astype(BF16)
    o_intra, b, rev = _hgrn_intra(q, k, lf, v_bf, tall_ref, lv_ref[...], n_levels)
    eb = jnp.exp(b)
    qs = q * eb
    ks_t = (k * jnp.exp(rev)).T
    eb_t = eb.T
    lane = lax.broadcasted_iota(jnp.int32, ks_t.shape, 1)
    row = lax.broadcasted_iota(jnp.int32, qs.shape, 0)
    o = o_intra
    for i in range(CHUNK // seq):
        s0 = s0_ref[i, 0]
        mine = (lane >= i * seq) & (lane < (i + 1) * seq)
        o_i = _dot(qs.astype(BF16), s0.astype(BF16))
        o = o + jnp.where((row >= i * seq) & (row < (i + 1) * seq), o_i, 0.0)
        ds = _dot(jnp.where(mine, ks_t, 0.0).astype(BF16), v_bf)
        sout_ref[i, 0] = s0 * eb_t[:, (i + 1) * seq - 1:(i + 1) * seq] + ds
    o_ref[...] = o


def _hgrn_decode(q, k, lf, v, s0, seq, consts):
    tall, lv, n_levels = consts
    n, d = q.shape
    heads = d // HGRN_HEAD
    per = CHUNK // seq
    blk = pl.BlockSpec((CHUNK, HGRN_HEAD), lambda g, h: (g, h))
    sblk = pl.BlockSpec((per, 1, HGRN_HEAD, HGRN_HEAD), lambda g, h: (g, h, 0, 0))
    full = lambda a: pl.BlockSpec(a.shape, lambda g, h: (0,) * a.ndim)
    return pl.pallas_call(
        functools.partial(_hgrn_decode_kernel, n_levels=n_levels, seq=seq),
        grid=(n // CHUNK, heads),
        in_specs=[blk, blk, blk, blk, sblk, full(tall), full(lv)],
        out_specs=[blk, sblk],
        out_shape=[jax.ShapeDtypeStruct((n, d), F32), jax.ShapeDtypeStruct(s0.shape, F32)],
        compiler_params=_params(("arbitrary", "arbitrary")),
    )(q, k, lf, v, s0, tall, lv)


def _hgrn_out_kernel(o_ref, gt_ref, x_ref, on_ref, w_ref, h_ref):
    o = o_ref[...]
    parts = []
    for h in range(o.shape[1] // HGRN_HEAD):
        oh = o[:, h * HGRN_HEAD:(h + 1) * HGRN_HEAD]
        parts.append(oh * lax.rsqrt(jnp.mean(oh * oh, axis=-1, keepdims=True) + EPS))
    gt = gt_ref[...]
    y = jnp.concatenate(parts, axis=-1) * on_ref[...] * (gt * _sigmoid(gt))
    h_ref[...] = x_ref[...] + _dot(y.astype(BF16), w_ref[...])


def _fox_out_kernel(o_ref, gt_ref, x_ref, w_ref, h_ref):
    y = o_ref[...] * _sigmoid(gt_ref[...])
    h_ref[...] = x_ref[...] + _dot(y.astype(BF16), w_ref[...])


def _mix_out(kernel, o, gt, x, *consts):
    n, d = x.shape
    tm = min(ROW_TILE, n)
    row = pl.BlockSpec((tm, d), lambda i: (i, 0))
    full = lambda a: pl.BlockSpec(a.shape, lambda i: (0,) * a.ndim)
    return pl.pallas_call(
        kernel,
        grid=(n // tm,),
        in_specs=[row, row, row] + [full(c) for c in consts],
        out_specs=row,
        out_shape=jax.ShapeDtypeStruct((n, d), F32),
        compiler_params=_params(("arbitrary",)),
    )(o, gt, x, *consts)


def _ffn_tail(h, u, u1, u2, gate, cw_ref, cb_ref, wout_ref):
    a = cb_ref[...] + cw_ref[0:1, :] * u2 + cw_ref[1:2, :] * u1 + cw_ref[2:3, :] * u
    y = a * _sigmoid(a) * gate
    return h + _dot(y.astype(BF16), wout_ref[...])


def _ffn_prompt_kernel(x_ref, g_ref, win_ref, cw_ref, cb_ref, wout_ref, h_ref, cs_ref, carry):
    j = pl.program_id(1)
    f = cw_ref.shape[1]

    @pl.when(j == 0)
    def _():
        carry[...] = jnp.zeros_like(carry)

    h = x_ref[...]
    tm = h.shape[0]
    ug = _dot(_rms(h, g_ref[...]).astype(BF16), win_ref[...])
    u = ug[:, :f]
    prev = carry[...]
    p0 = prev[SUBLANES - 2:SUBLANES - 1, :]
    p1 = prev[SUBLANES - 1:SUBLANES, :]
    row = lax.broadcasted_iota(jnp.int32, u.shape, 0)
    u1 = jnp.where(row == 0, p1, pltpu.roll(u, 1, axis=0))
    u2 = jnp.where(row == 0, p0, jnp.where(row == 1, p1, pltpu.roll(u, 2, axis=0)))
    h_ref[...] = _ffn_tail(h, u, u1, u2, ug[:, f:], cw_ref, cb_ref, wout_ref)
    carry[...] = u[tm - SUBLANES:, :]

    @pl.when(j == pl.num_programs(1) - 1)
    def _():
        cs_ref[0] = u[tm - (CONV_W - 1):, :]


def _ffn_prompt(x, g, win, cw, cb, wout, batch):
    n, d = x.shape
    f = cw.shape[1]
    t = n // batch
    tm = min(ROW_TILE, t)
    nj = t // tm
    row = pl.BlockSpec((tm, d), lambda b, j: (b * nj + j, 0))
    full = lambda a: pl.BlockSpec(a.shape, lambda b, j: (0,) * a.ndim)
    return pl.pallas_call(
        _ffn_prompt_kernel,
        grid=(batch, nj),
        in_specs=[row, full(g), full(win), full(cw), full(cb), full(wout)],
        out_specs=[row, pl.BlockSpec((1, CONV_W - 1, f), lambda b, j: (b, 0, 0))],
        out_shape=[jax.ShapeDtypeStruct((n, d), F32), jax.ShapeDtypeStruct((batch, CONV_W - 1, f), F32)],
        scratch_shapes=[pltpu.VMEM((SUBLANES, f), F32)],
        compiler_params=_params(("arbitrary", "arbitrary")),
    )(x, g, win, cw, cb, wout)


def _ffn_decode_kernel(x_ref, prev_ref, g_ref, win_ref, cw_ref, cb_ref, wout_ref, h_ref, cs_ref, *, seq):
    f = cw_ref.shape[1]
    h = x_ref[...]
    tm = h.shape[0]
    ug = _dot(_rms(h, g_ref[...]).astype(BF16), win_ref[...])
    u = ug[:, :f]
    u3 = u.reshape(tm // seq, seq, f)
    prev = prev_ref[...]
    p0 = prev[:, 0:1, :]
    p1 = prev[:, 1:2, :]
    row = lax.broadcasted_iota(jnp.int32, u3.shape, 1)
    u1 = jnp.where(row == 0, p1, pltpu.roll(u3, 1, axis=1))
    u2 = jnp.where(row == 0, p0, jnp.where(row == 1, p1, pltpu.roll(u3, 2, axis=1)))
    h_ref[...] = _ffn_tail(h, u, u1.reshape(tm, f), u2.reshape(tm, f), ug[:, f:], cw_ref, cb_ref, wout_ref)
    cs_ref[...] = u3[:, seq - (CONV_W - 1):, :]


def _ffn_decode(x, prev, g, win, cw, cb, wout, seq):
    n, d = x.shape
    f = cw.shape[1]
    tm = min(ROW_TILE, n)
    row = pl.BlockSpec((tm, d), lambda i: (i, 0))
    st = pl.BlockSpec((tm // seq, CONV_W - 1, f), lambda i: (i, 0, 0))
    full = lambda a: pl.BlockSpec(a.shape, lambda i: (0,) * a.ndim)
    return pl.pallas_call(
        functools.partial(_ffn_decode_kernel, seq=seq),
        grid=(n // tm,),
        in_specs=[row, st, full(g), full(win), full(cw), full(cb), full(wout)],
        out_specs=[row, st],
        out_shape=[jax.ShapeDtypeStruct((n, d), F32), jax.ShapeDtypeStruct(prev.shape, F32)],
        compiler_params=_params(("arbitrary",)),
    )(x, prev, g, win, cw, cb, wout)


def _shared_kv_kernel(x_ref, g_ref, wk_ref, wv_ref, wf_ref, bf_ref, kn_ref, bd_ref, tri_ref, init_ref,
                      k_ref, v_ref, lf_ref, c_ref, carry):
    j = pl.program_id(1)
    heads = lf_ref.shape[1]

    @pl.when(j == 0)
    def _():
        carry[...] = jnp.zeros_like(carry)

    hn = _rms(x_ref[...], g_ref[...]).astype(BF16)
    k = _dot(hn, wk_ref[...])
    k_ref[...] = _group_rms(k, bd_ref, FOX_HEAD) * kn_ref[...]
    v_ref[...] = _dot(hn, wv_ref[...])
    lf = _log_sigmoid(_dot(hn, wf_ref[...]) + bf_ref[...])
    c = _dot3_right(tri_ref[...], lf) + carry[...] + init_ref[...]
    carry[...] = c[c.shape[0] - 1:, :] - init_ref[c.shape[0] - 1:, :]
    lf_ref[...] = lf[:, :heads]
    c_ref[...] = c[:, :heads]


def _shared_kv(x, g, wk, wv, wf, bf, kn, bd, tri, init, groups, heads):
    n, d = x.shape
    tm = tri.shape[0]
    nj = n // groups // tm
    row = lambda w: pl.BlockSpec((tm, w), lambda b, j: (b * nj + j, 0))
    full = lambda a: pl.BlockSpec(a.shape, lambda b, j: (0,) * a.ndim)
    return pl.pallas_call(
        _shared_kv_kernel,
        grid=(groups, nj),
        in_specs=[row(d), full(g), full(wk), full(wv), full(wf), full(bf), full(kn), full(bd), full(tri),
                  row(LANES)],
        out_specs=[row(d), row(d), row(heads), row(heads)],
        out_shape=[jax.ShapeDtypeStruct((n, d), F32), jax.ShapeDtypeStruct((n, d), F32),
                   jax.ShapeDtypeStruct((n, heads), F32), jax.ShapeDtypeStruct((n, heads), F32)],
        scratch_shapes=[pltpu.VMEM((1, LANES), F32)],
        compiler_params=_params(("arbitrary", "arbitrary")),
    )(x, g, wk, wv, wf, bf, kn, bd, tri, init)


def _fox_qg_kernel(x_ref, g_ref, w_ref, qn_ref, bd_ref, q_ref, gt_ref):
    d = q_ref.shape[1]
    qg = _dot(_rms(x_ref[...], g_ref[...]).astype(BF16), w_ref[...])
    q_ref[...] = _group_rms(qg[:, :d], bd_ref, FOX_HEAD) * (qn_ref[...] * FOX_HEAD ** -0.5)
    gt_ref[...] = qg[:, d:]


def _fox_qg(x, g, w, qn, bd):
    n, d = x.shape
    tm = min(ROW_TILE, n)
    row = pl.BlockSpec((tm, d), lambda i: (i, 0))
    full = lambda a: pl.BlockSpec(a.shape, lambda i: (0,) * a.ndim)
    out = jax.ShapeDtypeStruct((n, d), F32)
    return pl.pallas_call(
        _fox_qg_kernel,
        grid=(n // tm,),
        in_specs=[row, full(g), full(w), full(qn), full(bd)],
        out_specs=[row, row],
        out_shape=[out, out],
        compiler_params=_params(("arbitrary",)),
    )(x, g, w, qn, bd)


def _attn_prompt_kernel(q_ref, k_ref, v_ref, cq_ref, ck_ref, o_ref):
    i = pl.program_id(2)
    tq = q_ref.shape[0]
    q = q_ref[...]
    lane = lax.broadcasted_iota(jnp.int32, q.shape, 1)
    qh = [jnp.where(lane < FOX_HEAD, q, 0.0).astype(BF16), jnp.where(lane >= FOX_HEAD, q, 0.0).astype(BF16)]
    cq = cq_ref[0, 0]
    qpos = i * tq + lax.broadcasted_iota(jnp.int32, (tq, tq), 0)
    kcol = lax.broadcasted_iota(jnp.int32, (tq, tq), 1)

    def body(j, carry):
        start = pl.multiple_of(j * tq, tq)
        kj = k_ref[pl.ds(start, tq), :].astype(BF16)
        vj = v_ref[pl.ds(start, tq), :].astype(BF16)
        keep = (start + kcol) <= qpos
        out = []
        for h in range(2):
            m, l, acc = carry[3 * h:3 * h + 3]
            s = _dot_nt(qh[h], kj) + (cq[:, h:h + 1] - ck_ref[0, 0, h:h + 1, pl.ds(start, tq)])
            s = jnp.where(keep, s, -jnp.inf)
            m_new = jnp.maximum(m, jnp.max(s, axis=-1, keepdims=True))
            p = jnp.exp(s - m_new)
            alpha = jnp.exp(m - m_new)
            out += [m_new, alpha * l + jnp.sum(p, axis=-1, keepdims=True),
                    alpha * acc + _dot(p.astype(BF16), vj)]
        return tuple(out)

    init = (jnp.full((tq, 1), -jnp.inf, F32), jnp.zeros((tq, 1), F32), jnp.zeros(q.shape, F32)) * 2
    res = lax.fori_loop(0, i + 1, body, init)
    o_ref[...] = jnp.where(lane < FOX_HEAD, res[2] / res[1], res[5] / res[4])


def _attn_prompt(q, k, v, cq, ck, batch):
    n, d = q.shape
    t = n // batch
    tq = min(ATTN_TILE, t)
    nq = t // tq
    pairs = d // LANES
    return pl.pallas_call(
        _attn_prompt_kernel,
        grid=(batch, pairs, nq),
        in_specs=[pl.BlockSpec((tq, LANES), lambda b, p, i: (b * nq + i, p)),
                  pl.BlockSpec((t, LANES), lambda b, p, i: (b, p)),
                  pl.BlockSpec((t, LANES), lambda b, p, i: (b, p)),
                  pl.BlockSpec((1, 1, tq, 2), lambda b, p, i: (b, p, i, 0)),
                  pl.BlockSpec((1, 1, 2, t), lambda b, p, i: (b, p, 0, 0))],
        out_specs=pl.BlockSpec((tq, LANES), lambda b, p, i: (b * nq + i, p)),
        out_shape=jax.ShapeDtypeStruct((n, d), F32),
        compiler_params=_params(("arbitrary", "arbitrary", "arbitrary")),
    )(q, k, v, cq, ck)


def _paged_cumsum_kernel(pt_ref, *refs):
    del pt_ref
    n_pages = (len(refs) - 3)
    u_ref, c_ref, tot_ref = refs[n_pages:]
    carry = jnp.zeros(refs[0].shape[1:], F32)
    for p in range(n_pages):
        c = _dot3_left(refs[p][0], u_ref[...]) + carry
        c_ref[0, :, p * PAGE:(p + 1) * PAGE] = c
        carry = jnp.broadcast_to(c[:, PAGE - 1:PAGE], c.shape)
    tot_ref[0] = carry


def _paged_cumsum(logf_t, page_table, upper):
    bd, n_pages = page_table.shape
    heads = logf_t.shape[1]
    page = lambda p: pl.BlockSpec((1, heads, PAGE), lambda s, pt: (pt[s * n_pages + p], 0, 0))
    grid_spec = pltpu.PrefetchScalarGridSpec(
        num_scalar_prefetch=1,
        grid=(bd,),
        in_specs=[page(p) for p in range(n_pages)] + [pl.BlockSpec(upper.shape, lambda s, pt: (0, 0))],
        out_specs=[pl.BlockSpec((1, heads, n_pages * PAGE), lambda s, pt: (s, 0, 0)),
                   pl.BlockSpec((1, heads, PAGE), lambda s, pt: (s, 0, 0))],
    )
    return pl.pallas_call(
        _paged_cumsum_kernel,
        grid_spec=grid_spec,
        out_shape=[jax.ShapeDtypeStruct((bd, heads, n_pages * PAGE), F32),
                   jax.ShapeDtypeStruct((bd, heads, PAGE), F32)],
        compiler_params=_params(("arbitrary",)),
    )(page_table.reshape(-1), *([logf_t] * n_pages), upper)


def _attn_decode_kernel(pt_ref, q_ref, kp_ref, vp_ref, kn_ref, vn_ref, cp_ref, cn_ref, cq_ref, hm_ref, o_ref,
                        qe, m_scr, l_scr, acc):
    del pt_ref
    p = pl.program_id(1)
    n_pages = pl.num_programs(1) - 1
    seq, d = q_ref.shape[1:]
    heads = hm_ref.shape[0]
    rows = seq * heads

    @pl.when(p == 0)
    def _():
        qe[...] = (q_ref[0][:, None, :] * hm_ref[...][None, :, :]).reshape(rows, d).astype(BF16)
        m_scr[...] = jnp.full_like(m_scr, -jnp.inf)
        l_scr[...] = jnp.zeros_like(l_scr)
        acc[...] = jnp.zeros_like(acc)

    def step(k_bf, v_bf, ck, keep):
        s = _dot_nt(qe[...], k_bf) + (cq_ref[0] - jnp.concatenate([ck] * seq, axis=0))
        if keep is not None:
            s = jnp.where(keep, s, -jnp.inf)
        m = m_scr[...]
        m_new = jnp.maximum(m, jnp.max(s, axis=-1, keepdims=True))
        pr = jnp.exp(s - m_new)
        alpha = jnp.exp(m - m_new)
        m_scr[...] = m_new
        l_scr[...] = alpha * l_scr[...] + jnp.sum(pr, axis=-1, keepdims=True)
        acc[...] = alpha * acc[...] + _dot(pr.astype(BF16), v_bf)

    @pl.when(p < n_pages)
    def _():
        step(kp_ref[0].astype(BF16), vp_ref[0].astype(BF16), cp_ref[0], None)

    @pl.when(p == n_pages)
    def _():
        pad = jnp.zeros((PAGE - seq, d), F32)
        k_new = jnp.concatenate([kn_ref[0], pad], axis=0).astype(BF16)
        v_new = jnp.concatenate([vn_ref[0], pad], axis=0).astype(BF16)
        t = lax.broadcasted_iota(jnp.int32, (rows, PAGE), 0) // heads
        key = lax.broadcasted_iota(jnp.int32, (rows, PAGE), 1)
        step(k_new, v_new, cn_ref[0], key <= t)
        out = (acc[...] / l_scr[...]).reshape(seq, heads, d) * hm_ref[...][None, :, :]
        o_ref[0] = jnp.sum(out, axis=1)


def _attn_decode(q, cache_k, cache_v, k_new, v_new, c_past_t, c_new_t, cq_col, head_mask, page_table):
    bd, seq, d = q.shape
    n_pages = page_table.shape[1]
    heads = head_mask.shape[0]
    rows = seq * heads
    last = n_pages - 1
    page = pl.BlockSpec((1, PAGE, d), lambda s, p, pt: (pt[s * n_pages + jnp.minimum(p, last)], 0, 0))
    tok = pl.BlockSpec((1, seq, d), lambda s, p, pt: (s, 0, 0))
    grid_spec = pltpu.PrefetchScalarGridSpec(
        num_scalar_prefetch=1,
        grid=(bd, n_pages + 1),
        in_specs=[tok, page, page, tok, tok,
                  pl.BlockSpec((1, heads, PAGE), lambda s, p, pt: (s, 0, jnp.minimum(p, last))),
                  pl.BlockSpec((1, heads, PAGE), lambda s, p, pt: (s, 0, 0)),
                  pl.BlockSpec((1, rows, 1), lambda s, p, pt: (s, 0, 0)),
                  pl.BlockSpec(head_mask.shape, lambda s, p, pt: (0, 0))],
        out_specs=tok,
        scratch_shapes=[pltpu.VMEM((rows, d), BF16), pltpu.VMEM((rows, 1), F32), pltpu.VMEM((rows, 1), F32),
                        pltpu.VMEM((rows, d), F32)],
    )
    return pl.pallas_call(
        _attn_decode_kernel,
        grid_spec=grid_spec,
        out_shape=jax.ShapeDtypeStruct((bd, seq, d), F32),
        compiler_params=_params(("arbitrary", "arbitrary")),
    )(page_table.reshape(-1), q, cache_k, cache_v, k_new, v_new, c_past_t, c_new_t, cq_col, head_mask)


def _trunk(x, seq, hgrn_s0, conv_s0, past, w):
    n, d = x.shape
    n_seq = n // seq
    depth = w["ffn_w_in"].shape[0]
    n_a = w["hgrn_w_in"].shape[0]
    heads = w["fox_b_f"].shape[0]
    prompt = past is None
    h = x
    new_hgrn, new_conv = [], []
    for layer in range(depth):
        if layer < n_a:
            q, k, lf, i, gt = _hgrn_in(h, w["norm_mix"][layer], w["hgrn_w_in"][layer], w["hgrn_lb_param"], layer)
            if prompt:
                o, s = _hgrn_prompt(q, k, lf, i, n_seq, w["hgrn_prompt_consts"])
            else:
                o, s = _hgrn_decode(q, k, lf, i, hgrn_s0[layer], seq, w["hgrn_decode_consts"])
            new_hgrn.append(s)
            h = _mix_out(_hgrn_out_kernel, o, gt, h, w["hgrn_onorm"][layer], w["hgrn_w_o"][layer])
        else:
            j = layer - n_a
            q, gt = _fox_qg(h, w["norm_mix"][layer], w["fox_w_qg"][j], w["q_norm"][j], w["bd_fox"])
            if prompt:
                o = _attn_prompt(q, k_sh, v_sh, cq, ck, n_seq)
            else:
                o = _attn_decode(q.reshape(n_seq, seq, d), past[0], past[1], k_sh.reshape(n_seq, seq, d),
                                 v_sh.reshape(n_seq, seq, d), c_past_t, c_new_t, cq_col, w["head_mask"],
                                 past[3]).reshape(n, d)
            h = _mix_out(_fox_out_kernel, o, gt, h, w["fox_w_o"][j])
        ffn = (w["norm_ffn"][layer], w["ffn_w_in"][layer], w["ffn_conv_w"][layer], w["ffn_conv_b"][layer],
               w["ffn_w_out"][layer])
        if prompt:
            h, cs = _ffn_prompt(h, *ffn, n_seq)
        else:
            h, cs = _ffn_decode(h, conv_s0[layer], *ffn, seq)
        new_conv.append(cs)
        if layer == n_a - 1:
            if prompt:
                init = jnp.zeros((n, LANES), F32)
                tri, groups = w["tri_prompt"], n_seq
            else:
                c_past_t, total = _paged_cumsum(past[2], past[3], w["upper"])
                init = jnp.pad(jnp.repeat(total[:, :, 0], seq, axis=0), ((0, 0), (0, LANES - heads)))
                tri, groups = w["tri_decode"], n // w["tri_decode"].shape[0]
            k_sh, v_sh, logf, c = _shared_kv(h, w["kv_norm"], w["w_k"], w["w_v"], w["w_f"], w["b_f"], w["k_norm"],
                                             w["bd_fox"], tri, init, groups, heads)
            if prompt:
                c4 = c.reshape(n_seq, seq, heads // 2, 2)
                cq = c4.transpose(0, 2, 1, 3)
                ck = c4.transpose(0, 2, 3, 1)
            else:
                c3 = c.reshape(n_seq, seq, heads)
                cq_col = c3.reshape(n_seq, seq * heads, 1)
                c_new_t = jnp.pad(c3.transpose(0, 2, 1), ((0, 0), (0, 0), (0, PAGE - seq)))
    return h, k_sh, v_sh, logf, jnp.stack(new_hgrn), jnp.stack(new_conv)


def kernel(x_prompt, x_sample, cache_k, cache_v, cache_logf, state_hgrn, state_conv, page_table, norm_mix, norm_ffn, hgrn_w_in, hgrn_lb_param, hgrn_onorm, hgrn_w_o, kv_norm, w_kvf, fox_b_f, k_norm, fox_w_qg, q_norm, fox_w_o, ffn_w_in, ffn_conv_w, ffn_conv_b, ffn_w_out):
    b, t, d = x_prompt.shape
    bd, ts, _ = x_sample.shape
    heads = fox_b_f.shape[0]
    n_phys, page = cache_k.shape[:2]
    assert page == PAGE and d % LANES == 0 and d // heads == FOX_HEAD
    assert t % ROW_TILE == 0 and (bd * ts) % ROW_TILE == 0 and CHUNK % ts == 0 and ROW_TILE % ts == 0
    assert ts >= CONV_W - 1

    row = lambda a: a.reshape(a.shape[:-1] + (1, a.shape[-1]))
    w = {
        "norm_mix": row(norm_mix), "norm_ffn": row(norm_ffn),
        "hgrn_w_in": hgrn_w_in.astype(BF16), "hgrn_lb_param": hgrn_lb_param,
        "hgrn_onorm": row(jnp.tile(hgrn_onorm, (1, d // HGRN_HEAD))), "hgrn_w_o": hgrn_w_o.astype(BF16),
        "kv_norm": row(kv_norm),
        "w_k": w_kvf[:, :d].astype(BF16), "w_v": w_kvf[:, d:2 * d].astype(BF16),
        "w_f": jnp.pad(w_kvf[:, 2 * d:], ((0, 0), (0, LANES - heads))).astype(BF16),
        "b_f": jnp.pad(fox_b_f, (0, LANES - heads)).reshape(1, LANES), "fox_b_f": fox_b_f,
        "k_norm": jnp.tile(k_norm, heads).reshape(1, d),
        "fox_w_qg": fox_w_qg.astype(BF16), "q_norm": row(jnp.tile(q_norm, (1, heads))),
        "fox_w_o": fox_w_o.astype(BF16),
        "ffn_w_in": ffn_w_in.astype(BF16), "ffn_conv_w": ffn_conv_w, "ffn_conv_b": row(ffn_conv_b),
        "ffn_w_out": ffn_w_out.astype(BF16),
        "hgrn_prompt_consts": _level_mats(CHUNK, CHUNK), "hgrn_decode_consts": _level_mats(CHUNK, ts),
        "bd_fox": _block_diag_ones(d, FOX_HEAD),
        "tri_prompt": _seg_lower_tri(ROW_TILE, ROW_TILE), "tri_decode": _seg_lower_tri(ROW_TILE, ts),
        "upper": _upper_tri(PAGE), "head_mask": _head_mask(heads, FOX_HEAD),
    }

    y_p, k_p, v_p, lf_p, hgrn_p, conv_p = _trunk(x_prompt.reshape(b * t, d), t, None, None, None, w)

    past = (cache_k.reshape(n_phys, page, d), cache_v.reshape(n_phys, page, d),
            cache_logf.astype(F32).transpose(0, 2, 1), page_table)
    y_s, k_s, v_s, lf_s, hgrn_s, conv_s = _trunk(x_sample.reshape(bd * ts, d), ts, state_hgrn, state_conv, past, w)

    hd = (heads, d // heads)
    return (y_p.reshape(b, t, d), y_s.reshape(bd, ts, d),
            k_p.reshape((b, t) + hd), v_p.reshape((b, t) + hd), lf_p.reshape(b, t, heads),
            k_s.reshape((bd, ts) + hd), v_s.reshape((bd, ts) + hd), lf_s.reshape(bd, ts, heads),
            hgrn_p, hgrn_s, conv_p, conv_s)
```

```python
import functools

import numpy as np
import jax
import jax.numpy as jnp
from jax import lax
from jax.experimental import pallas as pl
from jax.experimental.pallas import tpu as pltpu

F32 = jnp.float32
BF16 = jnp.bfloat16
EPS = 1e-6

LANES = 128
SUBLANES = 8
VMEM_LIMIT_BYTES = 56 * 1024 * 1024

HGRN_HEAD = 128
FOX_HEAD = 64
CONV_W = 3
CHUNK = 128
ROW_TILE = 256
ATTN_TILE = 512
ATTN_SUB = 256
PAGE = 128
DECODE_PAGES = 4
LOG2E = 1.4426950408889634


def _params(semantics):
    return pltpu.CompilerParams(dimension_semantics=semantics, vmem_limit_bytes=VMEM_LIMIT_BYTES)


def _const_spec(a):
    if isinstance(a, tuple):
        arr, layer = a
        return pl.BlockSpec((None,) + arr.shape[1:], lambda *_: (layer,) + (0,) * (arr.ndim - 1))
    return pl.BlockSpec(a.shape, lambda *_: (0,) * a.ndim)


def _arr(a):
    return a[0] if isinstance(a, tuple) else a


def _dot(a, b):
    return jnp.dot(a, b, preferred_element_type=F32)


def _dot_nt(a, b):
    return lax.dot_general(a, b, (((1,), (1,)), ((), ())), preferred_element_type=F32)


def _split3(x):
    hi = x.astype(BF16)
    r = x - hi.astype(F32)
    mid = r.astype(BF16)
    lo = (r - mid.astype(F32)).astype(BF16)
    return hi, mid, lo


def _dot3_right(t, x):
    hi, mid, lo = _split3(x)
    return _dot(t, hi) + _dot(t, mid) + _dot(t, lo)


def _dot3_left(x, t):
    hi, mid, lo = _split3(x)
    return _dot(hi, t) + _dot(mid, t) + _dot(lo, t)


def _rms(x, g):
    r = lax.rsqrt(jnp.mean(x * x, axis=-1, keepdims=True) + EPS)
    return x * r * g


def _sigmoid(x):
    return 1.0 / (1.0 + jnp.exp(-x))


def _log_sigmoid(x):
    return jnp.minimum(x, 0.0) - jnp.log(1.0 + jnp.exp(-jnp.abs(x)))


def _group_rms(x, bd_ref, width):
    ss = _dot((x * x).astype(BF16), bd_ref[...])
    return x * lax.rsqrt(ss * (1.0 / width) + EPS)


def _level_mats(rows, seg):
    t = np.arange(rows)[:, None]
    j = np.arange(rows)[None, :]
    same = (t // seg) == (j // seg)
    mats = [same & (j <= t), same & (j > t)]
    lv = np.full((rows, rows), -1, np.int32)
    lv[np.arange(rows), np.arange(rows)] = 0
    h, level = seg // 2, 1
    while h >= 1:
        blk = t // (2 * h)
        m = blk * 2 * h + h
        in_blk = (j // (2 * h)) == blk
        upper = t >= m
        mats.append(in_blk & np.where(upper, (j >= m) & (j <= t), (j >= t + 1) & (j <= m - 1)))
        lv[in_blk & upper & ((j % (2 * h)) < h)] = level
        h //= 2
        level += 1
    tall = np.concatenate([m.astype(np.float32) for m in mats], axis=0)
    return jnp.asarray(tall, BF16), jnp.asarray(lv), level - 1


def _block_diag_ones(n, width):
    i = np.arange(n)
    return jnp.asarray((i[:, None] // width) == (i[None, :] // width), BF16)


def _seg_lower_tri(n, seg):
    i = np.arange(n)
    return jnp.asarray(((i[:, None] // seg) == (i[None, :] // seg)) & (i[None, :] <= i[:, None]), BF16)


def _upper_tri(n):
    i = np.arange(n)
    return jnp.asarray(i[:, None] <= i[None, :], BF16)


def _head_mask(heads, width):
    lane = np.arange(heads * width)
    return jnp.asarray((lane[None, :] // width) == np.arange(heads)[:, None], F32)


def _hgrn_in_kernel(x_ref, g_ref, w_ref, lbp_ref, q_ref, k_ref, lf_ref, i_ref, gt_ref, *, layer):
    d = q_ref.shape[1]
    hn = _rms(x_ref[...], g_ref[...]).astype(BF16)
    proj = _dot(hn, w_ref[...])
    q = proj[:, :d]
    fr = proj[:, d:2 * d]
    q_ref[...] = q * _sigmoid(q)
    i_ref[...] = proj[:, 2 * d:3 * d]
    gt_ref[...] = proj[:, 3 * d:]
    if layer == 0:
        lf_ref[...] = _log_sigmoid(fr)
        k_ref[...] = _sigmoid(-fr)
    else:
        lbp = lbp_ref[...]
        e = jnp.exp(lbp - jnp.max(lbp, axis=0, keepdims=True))
        p = e / jnp.sum(e, axis=0, keepdims=True)
        lb = jnp.sum(p[1:layer + 1], axis=0, keepdims=True)
        lf_ref[...] = jnp.log(lb + (1.0 - lb) * _sigmoid(fr))
        k_ref[...] = (1.0 - lb) * _sigmoid(-fr)


def _hgrn_in(x, g, w, lbp, layer):
    n, d = x.shape
    tm = min(ROW_TILE, n)
    row = pl.BlockSpec((tm, d), lambda i: (i, 0))
    out = jax.ShapeDtypeStruct((n, d), F32)
    return pl.pallas_call(
        functools.partial(_hgrn_in_kernel, layer=layer),
        grid=(n // tm,),
        in_specs=[row, _const_spec(g), _const_spec(w), _const_spec(lbp)],
        out_specs=[row] * 5,
        out_shape=[out] * 5,
        compiler_params=_params(("arbitrary",)),
    )(x, _arr(g), _arr(w), lbp)


def _hgrn_intra(q, k, lf, v_bf, tall_ref, lv, n_levels):
    c = q.shape[0]
    hi, mid, lo = _split3(lf)
    d3 = _dot(tall_ref[...], jnp.concatenate([hi, mid, lo], axis=1))
    d = d3[:, :c] + d3[:, c:2 * c] + d3[:, 2 * c:]
    a = jnp.where(lv == 0, _dot_nt(q.astype(BF16), k.astype(BF16)), 0.0)
    for l in range(n_levels):
        e = jnp.exp(d[(2 + l) * c:(3 + l) * c])
        a = jnp.where(lv == l + 1, _dot_nt((q * e).astype(BF16), (k * e).astype(BF16)), a)
    return _dot(a.astype(BF16), v_bf), d[:c], d[c:2 * c]


def _hgrn_prompt_kernel(q_ref, k_ref, lf_ref, v_ref, tall_ref, lv_ref, o_ref, sout_ref, s_scr, *, n_levels):
    j = pl.program_id(2)

    @pl.when(j == 0)
    def _():
        s_scr[...] = jnp.zeros_like(s_scr)

    lv = lv_ref[...]
    for c in range(q_ref.shape[0] // CHUNK):
        rows = pl.ds(c * CHUNK, CHUNK)
        q, k, lf, v = q_ref[rows, :], k_ref[rows, :], lf_ref[rows, :], v_ref[rows, :]
        v_bf = v.astype(BF16)
        o_intra, b, rev = _hgrn_intra(q, k, lf, v_bf, tall_ref, lv, n_levels)
        eb = jnp.exp(b)
        s = s_scr[...]
        o_ref[rows, :] = o_intra + _dot((q * eb).astype(BF16), s.astype(BF16))
        ks_t = (k * jnp.exp(rev)).T.astype(BF16)
        decay = eb.T[:, CHUNK - 1:CHUNK]
        s_scr[...] = s * decay + _dot(ks_t, v_bf)

    @pl.when(j == pl.num_programs(2) - 1)
    def _():
        sout_ref[0, 0] = s_scr[...]


def _hgrn_prompt(q, k, lf, v, batch, consts):
    tall, lv, n_levels = consts
    n, d = q.shape
    heads = d // HGRN_HEAD
    t = n // batch
    tb = min(4 * CHUNK, t)
    nj = t // tb
    blk = pl.BlockSpec((tb, HGRN_HEAD), lambda b, h, j: (b * nj + j, h))
    full = lambda a: pl.BlockSpec(a.shape, lambda b, h, j: (0,) * a.ndim)
    return pl.pallas_call(
        functools.partial(_hgrn_prompt_kernel, n_levels=n_levels),
        grid=(batch, heads, nj),
        in_specs=[blk, blk, blk, blk, full(tall), full(lv)],
        out_specs=[blk, pl.BlockSpec((1, 1, HGRN_HEAD, HGRN_HEAD), lambda b, h, j: (b, h, 0, 0))],
        out_shape=[jax.ShapeDtypeStruct((n, d), F32),
                   jax.ShapeDtypeStruct((batch, heads, HGRN_HEAD, HGRN_HEAD), F32)],
        scratch_shapes=[pltpu.VMEM((HGRN_HEAD, HGRN_HEAD), F32)],
        compiler_params=_params(("arbitrary", "arbitrary", "arbitrary")),
    )(q, k, lf, v, tall, lv)


def _hgrn_decode_kernel(q_ref, k_ref, lf_ref, v_ref, s0_ref, tall_ref, lv_ref, o_ref, sout_ref, *, n_levels, seq):
    q, k, lf, v = q_ref[...], k_ref[...], lf_ref[...], v_ref[...]
    v_bf = v.astype(BF16)
    o_intra, b, rev = _hgrn_intra(q, k, lf, v_bf, tall_ref, lv_ref[...], n_levels)
    eb = jnp.exp(b)
    qs = q * eb
    ks_t = (k * jnp.exp(rev)).T
    eb_t = eb.T
    lane = lax.broadcasted_iota(jnp.int32, ks_t.shape, 1)
    row = lax.broadcasted_iota(jnp.int32, qs.shape, 0)
    o = o_intra
    for i in range(CHUNK // seq):
        s0 = s0_ref[i, 0]
        mine = (lane >= i * seq) & (lane < (i + 1) * seq)
        o_i = _dot(qs.astype(BF16), s0.astype(BF16))
        o = o + jnp.where((row >= i * seq) & (row < (i + 1) * seq), o_i, 0.0)
        ds = _dot(jnp.where(mine, ks_t, 0.0).astype(BF16), v_bf)
        sout_ref[i, 0] = s0 * eb_t[:, (i + 1) * seq - 1:(i + 1) * seq] + ds
    o_ref[...] = o


def _hgrn_decode(q, k, lf, v, s0_all, layer, seq, consts):
    tall, lv, n_levels = consts
    n, d = q.shape
    heads = d // HGRN_HEAD
    per = CHUNK // seq
    blk = pl.BlockSpec((CHUNK, HGRN_HEAD), lambda g, h: (g, h))
    sblk = pl.BlockSpec((per, 1, HGRN_HEAD, HGRN_HEAD), lambda g, h: (g, h, 0, 0))
    s0blk = pl.BlockSpec((None, per, 1, HGRN_HEAD, HGRN_HEAD), lambda g, h: (layer, g, h, 0, 0))
    return pl.pallas_call(
        functools.partial(_hgrn_decode_kernel, n_levels=n_levels, seq=seq),
        grid=(n // CHUNK, heads),
        in_specs=[blk, blk, blk, blk, s0blk, _const_spec(tall), _const_spec(lv)],
        out_specs=[blk, sblk],
        out_shape=[jax.ShapeDtypeStruct((n, d), F32), jax.ShapeDtypeStruct(s0_all.shape[1:], F32)],
        compiler_params=_params(("arbitrary", "arbitrary")),
    )(q, k, lf, v, s0_all, tall, lv)


def _hgrn_out_kernel(o_ref, gt_ref, x_ref, on_ref, w_ref, h_ref):
    o = o_ref[...]
    parts = []
    for h in range(o.shape[1] // HGRN_HEAD):
        oh = o[:, h * HGRN_HEAD:(h + 1) * HGRN_HEAD]
        parts.append(oh * lax.rsqrt(jnp.mean(oh * oh, axis=-1, keepdims=True) + EPS))
    gt = gt_ref[...]
    y = jnp.concatenate(parts, axis=-1) * on_ref[...] * (gt * _sigmoid(gt))
    h_ref[...] = x_ref[...] + _dot(y.astype(BF16), w_ref[...])


def _fox_out_kernel(o_ref, gt_ref, x_ref, w_ref, h_ref):
    y = o_ref[...] * _sigmoid(gt_ref[...])
    h_ref[...] = x_ref[...] + _dot(y.astype(BF16), w_ref[...])


def _mix_out(kernel, o, gt, x, *consts):
    n, d = x.shape
    tm = min(ROW_TILE, n)
    row = pl.BlockSpec((tm, d), lambda i: (i, 0))
    return pl.pallas_call(
        kernel,
        grid=(n // tm,),
        in_specs=[row, row, row] + [_const_spec(c) for c in consts],
        out_specs=row,
        out_shape=jax.ShapeDtypeStruct((n, d), F32),
        compiler_params=_params(("arbitrary",)),
    )(o, gt, x, *[_arr(c) for c in consts])


def _ffn_tail(h, u, u1, u2, gate, cw_ref, cb_ref, wout_ref):
    a = cb_ref[...] + cw_ref[0:1, :] * u2 + cw_ref[1:2, :] * u1 + cw_ref[2:3, :] * u
    y = a * _sigmoid(a) * gate
    return h + _dot(y.astype(BF16), wout_ref[...])


def _ffn_prompt_kernel(x_ref, g_ref, win_ref, cw_ref, cb_ref, wout_ref, h_ref, cs_ref, carry):
    j = pl.program_id(1)
    f = cw_ref.shape[1]

    @pl.when(j == 0)
    def _():
        carry[...] = jnp.zeros_like(carry)

    h = x_ref[...]
    tm = h.shape[0]
    ug = _dot(_rms(h, g_ref[...]).astype(BF16), win_ref[...])
    u = ug[:, :f]
    prev = carry[...]
    p0 = prev[SUBLANES - 2:SUBLANES - 1, :]
    p1 = prev[SUBLANES - 1:SUBLANES, :]
    row = lax.broadcasted_iota(jnp.int32, u.shape, 0)
    u1 = jnp.where(row == 0, p1, pltpu.roll(u, 1, axis=0))
    u2 = jnp.where(row == 0, p0, jnp.where(row == 1, p1, pltpu.roll(u, 2, axis=0)))
    h_ref[...] = _ffn_tail(h, u, u1, u2, ug[:, f:], cw_ref, cb_ref, wout_ref)
    carry[...] = u[tm - SUBLANES:, :]

    @pl.when(j == pl.num_programs(1) - 1)
    def _():
        cs_ref[0] = u[tm - (CONV_W - 1):, :]


def _ffn_prompt(x, g, win, cw, cb, wout, batch):
    n, d = x.shape
    f = _arr(cw).shape[-1]
    t = n // batch
    tm = min(ROW_TILE, t)
    nj = t // tm
    row = pl.BlockSpec((tm, d), lambda b, j: (b * nj + j, 0))
    consts = [g, win, cw, cb, wout]
    return pl.pallas_call(
        _ffn_prompt_kernel,
        grid=(batch, nj),
        in_specs=[row] + [_const_spec(c) for c in consts],
        out_specs=[row, pl.BlockSpec((1, CONV_W - 1, f), lambda b, j: (b, 0, 0))],
        out_shape=[jax.ShapeDtypeStruct((n, d), F32), jax.ShapeDtypeStruct((batch, CONV_W - 1, f), F32)],
        scratch_shapes=[pltpu.VMEM((SUBLANES, f), F32)],
        compiler_params=_params(("arbitrary", "arbitrary")),
    )(x, *[_arr(c) for c in consts])


def _ffn_decode_kernel(x_ref, prev_ref, g_ref, win_ref, cw_ref, cb_ref, wout_ref, h_ref, cs_ref, *, seq):
    f = cw_ref.shape[1]
    h = x_ref[...]
    tm = h.shape[0]
    ug = _dot(_rms(h, g_ref[...]).astype(BF16), win_ref[...])
    u = ug[:, :f]
    u3 = u.reshape(tm // seq, seq, f)
    prev = prev_ref[...]
    p0 = prev[:, 0:1, :]
    p1 = prev[:, 1:2, :]
    row = lax.broadcasted_iota(jnp.int32, u3.shape, 1)
    u1 = jnp.where(row == 0, p1, pltpu.roll(u3, 1, axis=1))
    u2 = jnp.where(row == 0, p0, jnp.where(row == 1, p1, pltpu.roll(u3, 2, axis=1)))
    h_ref[...] = _ffn_tail(h, u, u1.reshape(tm, f), u2.reshape(tm, f), ug[:, f:], cw_ref, cb_ref, wout_ref)
    cs_ref[...] = u3[:, seq - (CONV_W - 1):, :]


def _ffn_decode(x, prev_all, layer, g, win, cw, cb, wout, seq):
    n, d = x.shape
    f = _arr(cw).shape[-1]
    tm = min(ROW_TILE, n)
    row = pl.BlockSpec((tm, d), lambda i: (i, 0))
    st = pl.BlockSpec((tm // seq, CONV_W - 1, f), lambda i: (i, 0, 0))
    st_in = pl.BlockSpec((None, tm // seq, CONV_W - 1, f), lambda i: (layer, i, 0, 0))
    consts = [g, win, cw, cb, wout]
    return pl.pallas_call(
        functools.partial(_ffn_decode_kernel, seq=seq),
        grid=(n // tm,),
        in_specs=[row, st_in] + [_const_spec(c) for c in consts],
        out_specs=[row, st],
        out_shape=[jax.ShapeDtypeStruct((n, d), F32), jax.ShapeDtypeStruct(prev_all.shape[1:], F32)],
        compiler_params=_params(("arbitrary",)),
    )(x, prev_all, *[_arr(c) for c in consts])


def _aug_consts(heads):
    width = heads * LANES
    pq, pk = np.zeros((3 * LANES, width), np.float32), np.zeros((3 * LANES, width), np.float32)
    oq, ok, real = (np.zeros((1, width), np.float32) for _ in range(3))
    for h in range(heads):
        e = h % 2
        free = LANES * h + FOX_HEAD * (1 - e)
        real[0, LANES * h + FOX_HEAD * e:LANES * h + FOX_HEAD * (e + 1)] = 1.0
        for x in range(3):
            pq[LANES * x + h, free + x] = 1.0
            pk[LANES * x + h, free + 3 + x] = -1.0
            oq[0, free + 3 + x] = 1.0
            ok[0, free + x] = 1.0
    return (jnp.asarray(real), (jnp.asarray(pq, BF16), jnp.asarray(oq)), (jnp.asarray(pk, BF16), jnp.asarray(ok)))


def _head_slots(x, real, bias):
    pairs = x.shape[1] // LANES
    dup = jnp.concatenate([x[:, LANES * p:LANES * (p + 1)] for p in range(pairs) for _ in range(2)], axis=1)
    return jnp.where(real > 0.0, dup, bias).astype(BF16)


def _bias_slots(c, p_ref, ones_ref):
    hi, mid, lo = _split3(c * LOG2E)
    return _dot(jnp.concatenate([hi, mid, lo], axis=1), p_ref[...]) + ones_ref[...]


def _shared_kv_kernel(x_ref, g_ref, wk_ref, wv_ref, wf_ref, bf_ref, kn_ref, bd_ref, tri_ref, init_ref, *rest, aug):
    if aug:
        real_ref, pk_ref, ok_ref, k_ref, v_ref, lf_ref, c_ref, ka_ref, va_ref, carry = rest
    else:
        k_ref, v_ref, lf_ref, c_ref, carry = rest
    j = pl.program_id(1)
    heads = lf_ref.shape[1]

    @pl.when(j == 0)
    def _():
        carry[...] = jnp.zeros_like(carry)

    hn = _rms(x_ref[...], g_ref[...]).astype(BF16)
    k = _group_rms(_dot(hn, wk_ref[...]), bd_ref, FOX_HEAD) * kn_ref[...]
    v = _dot(hn, wv_ref[...])
    k_ref[...] = k
    v_ref[...] = v
    lf = _log_sigmoid(_dot(hn, wf_ref[...]) + bf_ref[...])
    c = _dot3_right(tri_ref[...], lf) + carry[...] + init_ref[...]
    carry[...] = c[c.shape[0] - 1:, :] - init_ref[c.shape[0] - 1:, :]
    lf_ref[...] = lf[:, :heads]
    c_ref[...] = c
    if aug:
        real = real_ref[...]
        ka_ref[...] = _head_slots(k, real, _bias_slots(c, pk_ref, ok_ref))
        va_ref[...] = _head_slots(v, real, 1.0)


def _shared_kv(x, g, wk, wv, wf, bf, kn, bd, tri, init, groups, heads, aug=None):
    n, d = x.shape
    tm = tri.shape[0]
    nj = n // groups // tm
    row = lambda w: pl.BlockSpec((tm, w), lambda b, j: (b * nj + j, 0))
    full = lambda a: pl.BlockSpec(a.shape, lambda b, j: (0,) * a.ndim)
    consts = [g, wk, wv, wf, bf, kn, bd, tri]
    out_specs = [row(d), row(d), row(heads), row(LANES)]
    out_shape = [jax.ShapeDtypeStruct((n, d), F32), jax.ShapeDtypeStruct((n, d), F32),
                 jax.ShapeDtypeStruct((n, heads), F32), jax.ShapeDtypeStruct((n, LANES), F32)]
    extra = []
    if aug is not None:
        real, _, (pk, ok) = aug
        extra = [real, pk, ok]
        out_specs += [row(heads * LANES)] * 2
        out_shape += [jax.ShapeDtypeStruct((n, heads * LANES), BF16)] * 2
    return pl.pallas_call(
        functools.partial(_shared_kv_kernel, aug=aug is not None),
        grid=(groups, nj),
        in_specs=[row(d)] + [full(a) for a in consts] + [row(LANES)] + [full(a) for a in extra],
        out_specs=out_specs,
        out_shape=out_shape,
        scratch_shapes=[pltpu.VMEM((1, LANES), F32)],
        compiler_params=_params(("arbitrary", "arbitrary")),
    )(x, *consts, init, *extra)


def _fox_qg_kernel(x_ref, g_ref, w_ref, qn_ref, bd_ref, *rest, aug):
    d = x_ref.shape[1]
    qg = _dot(_rms(x_ref[...], g_ref[...]).astype(BF16), w_ref[...])
    q = _group_rms(qg[:, :d], bd_ref, FOX_HEAD) * (qn_ref[...] * FOX_HEAD ** -0.5)
    if aug:
        c_ref, real_ref, pq_ref, oq_ref, q_ref, gt_ref = rest
        q_ref[...] = _head_slots(q * LOG2E, real_ref[...], _bias_slots(c_ref[...], pq_ref, oq_ref))
    else:
        q_ref, gt_ref = rest
        q_ref[...] = q
    gt_ref[...] = qg[:, d:]


def _fox_qg(x, g, w, qn, bd, c=None, aug=None):
    n, d = x.shape
    tm = min(ROW_TILE, n)
    row = lambda w_: pl.BlockSpec((tm, w_), lambda i: (i, 0))
    full = lambda a: pl.BlockSpec(a.shape, lambda i: (0,) * a.ndim)
    consts = [g, w, qn, bd]
    if aug is None:
        extra, extra_specs = [], []
        q_spec, q_shape = row(d), jax.ShapeDtypeStruct((n, d), F32)
    else:
        real, (pq, oq), _ = aug
        extra, extra_specs = [c, real, pq, oq], [row(LANES), full(real), full(pq), full(oq)]
        width = real.shape[1]
        q_spec, q_shape = row(width), jax.ShapeDtypeStruct((n, width), BF16)
    return pl.pallas_call(
        functools.partial(_fox_qg_kernel, aug=aug is not None),
        grid=(n // tm,),
        in_specs=[row(d)] + [_const_spec(a) for a in consts] + extra_specs,
        out_specs=[q_spec, row(d)],
        out_shape=[q_shape, jax.ShapeDtypeStruct((n, d), F32)],
        compiler_params=_params(("arbitrary",)),
    )(x, *[_arr(a) for a in consts], *extra)


def _attn_prompt_kernel(q_ref, k_ref, v_ref, o_ref, m_scr, acc_scr):
    i = pl.program_id(2)
    tq = q_ref.shape[0]
    sub = ATTN_SUB
    nsub = tq // sub
    m_scr[...] = jnp.full_like(m_scr, -jnp.inf)
    acc_scr[...] = jnp.zeros_like(acc_scr)

    def slot(ref, rows, e):
        return ref[rows, e * LANES:(e + 1) * LANES]

    def update(c, s, v_bf):
        m = m_scr[c]
        m_new = jnp.maximum(m, jnp.max(s, axis=-1, keepdims=True))
        p = jnp.exp2(s - m_new)
        m_scr[c] = m_new
        acc_scr[c] = jnp.exp2(m - m_new) * acc_scr[c] + _dot(p.astype(BF16), v_bf)

    ahead = lax.broadcasted_iota(jnp.int32, (sub, tq), 1) - lax.broadcasted_iota(jnp.int32, (sub, tq), 0)

    def body(j, carry):
        rows = pl.ds(pl.multiple_of(j * tq, tq), tq)
        for e in range(2):
            kj, vj = slot(k_ref, rows, e), slot(v_ref, rows, e)
            for a in range(nsub):
                s = _dot_nt(slot(q_ref, slice(a * sub, (a + 1) * sub), e), kj)
                s = jnp.where(ahead <= (i - j) * tq + a * sub, s, -jnp.inf)
                update(e * nsub + a, s, vj)
        return carry

    lax.fori_loop(0, i + 1, body, 0)

    lane = lax.broadcasted_iota(jnp.int32, (sub, LANES), 1)
    for a in range(nsub):
        outs = []
        for e in range(2):
            acc = acc_scr[e * nsub + a]
            outs.append(acc / pltpu.roll(acc, FOX_HEAD, axis=1))
        o_ref[a * sub:(a + 1) * sub, :] = jnp.where(lane < FOX_HEAD, outs[0], outs[1])


def _attn_prompt(qa, ka, va, batch):
    n, width = qa.shape
    t = n // batch
    tq = min(ATTN_TILE, t)
    nq = t // tq
    pairs = width // (2 * LANES)
    chains = 2 * (tq // ATTN_SUB)
    return pl.pallas_call(
        _attn_prompt_kernel,
        grid=(batch, pairs, nq),
        in_specs=[pl.BlockSpec((tq, 2 * LANES), lambda b, p, i: (b * nq + i, p)),
                  pl.BlockSpec((t, 2 * LANES), lambda b, p, i: (b, p)),
                  pl.BlockSpec((t, 2 * LANES), lambda b, p, i: (b, p))],
        out_specs=pl.BlockSpec((tq, LANES), lambda b, p, i: (b * nq + i, p)),
        out_shape=jax.ShapeDtypeStruct((n, pairs * LANES), F32),
        scratch_shapes=[pltpu.VMEM((chains, ATTN_SUB, 1), F32), pltpu.VMEM((chains, ATTN_SUB, LANES), F32)],
        compiler_params=_params(("arbitrary", "arbitrary", "arbitrary")),
    )(qa, ka, va)


def _paged_cumsum_kernel(pt_ref, *refs):
    del pt_ref
    n_pages = (len(refs) - 3)
    u_ref, c_ref, tot_ref = refs[n_pages:]
    carry = jnp.zeros(refs[0].shape[1:], F32)
    for p in range(n_pages):
        c = _dot3_left(refs[p][0], u_ref[...]) + carry
        c_ref[0, :, p * PAGE:(p + 1) * PAGE] = c
        carry = jnp.broadcast_to(c[:, PAGE - 1:PAGE], c.shape)
    tot_ref[0] = carry


def _paged_cumsum(logf_t, page_table, upper):
    bd, n_pages = page_table.shape
    heads = logf_t.shape[1]
    page = lambda p: pl.BlockSpec((1, heads, PAGE), lambda s, pt: (pt[s * n_pages + p], 0, 0))
    grid_spec = pltpu.PrefetchScalarGridSpec(
        num_scalar_prefetch=1,
        grid=(bd,),
        in_specs=[page(p) for p in range(n_pages)] + [pl.BlockSpec(upper.shape, lambda s, pt: (0, 0))],
        out_specs=[pl.BlockSpec((1, heads, n_pages * PAGE), lambda s, pt: (s, 0, 0)),
                   pl.BlockSpec((1, heads, PAGE), lambda s, pt: (s, 0, 0))],
    )
    return pl.pallas_call(
        _paged_cumsum_kernel,
        grid_spec=grid_spec,
        out_shape=[jax.ShapeDtypeStruct((bd, heads, n_pages * PAGE), F32),
                   jax.ShapeDtypeStruct((bd, heads, PAGE), F32)],
        compiler_params=_params(("arbitrary",)),
    )(page_table.reshape(-1), *([logf_t] * n_pages), upper)


def _attn_decode_kernel(pt_ref, q_ref, *refs, group):
    del pt_ref
    kp, vp = refs[:group], refs[group:2 * group]
    kn_ref, vn_ref, cp_ref, cn_ref, cq_ref, hm_ref, o_ref, qe, m_scr, l_scr, acc = refs[2 * group:]
    j = pl.program_id(1)
    seq, d = q_ref.shape[1:]
    heads = hm_ref.shape[0]
    rows = seq * heads

    @pl.when(j == 0)
    def _():
        qe[...] = (q_ref[0][:, None, :] * hm_ref[...][None, :, :]).reshape(rows, d).astype(BF16)
        m_scr[...] = jnp.full_like(m_scr, -jnp.inf)
        l_scr[...] = jnp.zeros_like(l_scr)
        acc[...] = jnp.zeros_like(acc)

    def update(s, ck, keep, pv):
        s = s + (cq_ref[0] - jnp.concatenate([ck] * seq, axis=0))
        if keep is not None:
            s = jnp.where(keep, s, -jnp.inf)
        m = m_scr[...]
        m_new = jnp.maximum(m, jnp.max(s, axis=-1, keepdims=True))
        pr = jnp.exp(s - m_new)
        alpha = jnp.exp(m - m_new)
        m_scr[...] = m_new
        l_scr[...] = alpha * l_scr[...] + jnp.sum(pr, axis=-1, keepdims=True)
        acc[...] = alpha * acc[...] + pv(pr.astype(BF16))

    q = qe[...]
    s = jnp.concatenate([_dot(q, kp[g][0].astype(BF16)) for g in range(group)], axis=1)

    def pv_pages(pr):
        out = _dot_nt(pr[:, :PAGE], vp[0][0].astype(BF16))
        for g in range(1, group):
            out += _dot_nt(pr[:, g * PAGE:(g + 1) * PAGE], vp[g][0].astype(BF16))
        return out

    update(s, cp_ref[0], None, pv_pages)

    @pl.when(j == pl.num_programs(1) - 1)
    def _():
        pad = jnp.zeros((PAGE - seq, d), F32)
        k_new = jnp.concatenate([kn_ref[0], pad], axis=0).astype(BF16)
        v_new = jnp.concatenate([vn_ref[0], pad], axis=0).astype(BF16)
        t = lax.broadcasted_iota(jnp.int32, (rows, PAGE), 0) // heads
        key = lax.broadcasted_iota(jnp.int32, (rows, PAGE), 1)
        update(_dot_nt(q, k_new), cn_ref[0], key <= t, lambda pr: _dot(pr, v_new))
        out = (acc[...] / l_scr[...]).reshape(seq, heads, d) * hm_ref[...][None, :, :]
        o_ref[0] = jnp.sum(out, axis=1)


def _attn_decode(q, cache_kt, cache_vt, k_new, v_new, c_past_t, c_new_t, cq_col, head_mask, page_table):
    bd, seq, d = q.shape
    n_pages = page_table.shape[1]
    heads = head_mask.shape[0]
    rows = seq * heads
    group = DECODE_PAGES
    page = lambda g: pl.BlockSpec((1, d, PAGE), lambda s, j, pt: (pt[s * n_pages + j * group + g], 0, 0))
    tok = pl.BlockSpec((1, seq, d), lambda s, j, pt: (s, 0, 0))
    grid_spec = pltpu.PrefetchScalarGridSpec(
        num_scalar_prefetch=1,
        grid=(bd, n_pages // group),
        in_specs=[tok] + [page(g) for g in range(group)] + [page(g) for g in range(group)] + [
            tok, tok,
            pl.BlockSpec((1, heads, group * PAGE), lambda s, j, pt: (s, 0, j)),
            pl.BlockSpec((1, heads, PAGE), lambda s, j, pt: (s, 0, 0)),
            pl.BlockSpec((1, rows, 1), lambda s, j, pt: (s, 0, 0)),
            pl.BlockSpec(head_mask.shape, lambda s, j, pt: (0, 0))],
        out_specs=tok,
        scratch_shapes=[pltpu.VMEM((rows, d), BF16), pltpu.VMEM((rows, 1), F32), pltpu.VMEM((rows, 1), F32),
                        pltpu.VMEM((rows, d), F32)],
    )
    return pl.pallas_call(
        functools.partial(_attn_decode_kernel, group=group),
        grid_spec=grid_spec,
        out_shape=jax.ShapeDtypeStruct((bd, seq, d), F32),
        compiler_params=_params(("arbitrary", "arbitrary")),
    )(page_table.reshape(-1), q, *([cache_kt] * group), *([cache_vt] * group), k_new, v_new, c_past_t, c_new_t,
      cq_col, head_mask)


def _trunk(x, seq, hgrn_s0, conv_s0, past, w):
    n, d = x.shape
    n_seq = n // seq
    depth = w["ffn_w_in"].shape[0]
    n_a = w["hgrn_w_in"].shape[0]
    heads = w["fox_b_f"].shape[0]
    prompt = past is None
    h = x
    new_hgrn, new_conv = [], []
    for layer in range(depth):
        if layer < n_a:
            q, k, lf, i, gt = _hgrn_in(h, (w["norm_mix"], layer), (w["hgrn_w_in"], layer), w["hgrn_lb_param"], layer)
            if prompt:
                o, s = _hgrn_prompt(q, k, lf, i, n_seq, w["hgrn_prompt_consts"])
            else:
                o, s = _hgrn_decode(q, k, lf, i, hgrn_s0, layer, seq, w["hgrn_decode_consts"])
            new_hgrn.append(s)
            h = _mix_out(_hgrn_out_kernel, o, gt, h, (w["hgrn_onorm"], layer), (w["hgrn_w_o"], layer))
        else:
            j = layer - n_a
            qg = (h, (w["norm_mix"], layer), (w["fox_w_qg"], j), (w["q_norm"], j), w["bd_fox"])
            if prompt:
                qa, gt = _fox_qg(*qg, c, w["aug"])
                o = _attn_prompt(qa, ka, va, n_seq)
            else:
                q, gt = _fox_qg(*qg)
                o = _attn_decode(q.reshape(n_seq, seq, d), past[0], past[1], k_sh.reshape(n_seq, seq, d),
                                 v_sh.reshape(n_seq, seq, d), c_past_t, c_new_t, cq_col, w["head_mask"],
                                 past[3]).reshape(n, d)
            h = _mix_out(_fox_out_kernel, o, gt, h, (w["fox_w_o"], j))
        ffn = [(w[name], layer) for name in ("norm_ffn", "ffn_w_in", "ffn_conv_w", "ffn_conv_b", "ffn_w_out")]
        if prompt:
            h, cs = _ffn_prompt(h, *ffn, n_seq)
        else:
            h, cs = _ffn_decode(h, conv_s0, layer, *ffn, seq)
        new_conv.append(cs)
        if layer == n_a - 1:
            if prompt:
                init = jnp.zeros((n, LANES), F32)
                tri, groups = w["tri_prompt"], n_seq
            else:
                c_past_t, total = _paged_cumsum(past[2], past[3], w["upper"])
                init = jnp.pad(jnp.repeat(total[:, :, 0], seq, axis=0), ((0, 0), (0, LANES - heads)))
                tri, groups = w["tri_decode"], n // w["tri_decode"].shape[0]
            kv = _shared_kv(h, w["kv_norm"], w["w_k"], w["w_v"], w["w_f"], w["b_f"], w["k_norm"], w["bd_fox"], tri,
                            init, groups, heads, w["aug"] if prompt else None)
            k_sh, v_sh, logf, c = kv[:4]
            if prompt:
                ka, va = kv[4:]
            else:
                c3 = c[:, :heads].reshape(n_seq, seq, heads)
                cq_col = c3.reshape(n_seq, seq * heads, 1)
                c_new_t = jnp.pad(c3.transpose(0, 2, 1), ((0, 0), (0, 0), (0, PAGE - seq)))
    return h, k_sh, v_sh, logf, jnp.stack(new_hgrn), jnp.stack(new_conv)


def kernel(x_prompt, x_sample, cache_k, cache_v, cache_logf, state_hgrn, state_conv, page_table, norm_mix, norm_ffn, hgrn_w_in, hgrn_lb_param, hgrn_onorm, hgrn_w_o, kv_norm, w_kvf, fox_b_f, k_norm, fox_w_qg, q_norm, fox_w_o, ffn_w_in, ffn_conv_w, ffn_conv_b, ffn_w_out):
    b, t, d = x_prompt.shape
    bd, ts, _ = x_sample.shape
    heads = fox_b_f.shape[0]
    n_phys, page = cache_k.shape[:2]
    assert page == PAGE and d % LANES == 0 and d // heads == FOX_HEAD
    assert t % ATTN_TILE == 0 and page_table.shape[1] % DECODE_PAGES == 0
    assert t % ROW_TILE == 0 and (bd * ts) % ROW_TILE == 0 and CHUNK % ts == 0 and ROW_TILE % ts == 0
    assert ts >= CONV_W - 1

    row = lambda a: a.reshape(a.shape[:-1] + (1, a.shape[-1]))
    w = {
        "norm_mix": row(norm_mix), "norm_ffn": row(norm_ffn),
        "hgrn_w_in": hgrn_w_in.astype(BF16), "hgrn_lb_param": hgrn_lb_param,
        "hgrn_onorm": row(jnp.tile(hgrn_onorm, (1, d // HGRN_HEAD))), "hgrn_w_o": hgrn_w_o.astype(BF16),
        "kv_norm": row(kv_norm),
        "w_k": w_kvf[:, :d].astype(BF16), "w_v": w_kvf[:, d:2 * d].astype(BF16),
        "w_f": jnp.pad(w_kvf[:, 2 * d:], ((0, 0), (0, LANES - heads))).astype(BF16),
        "b_f": jnp.pad(fox_b_f, (0, LANES - heads)).reshape(1, LANES), "fox_b_f": fox_b_f,
        "k_norm": jnp.tile(k_norm, heads).reshape(1, d),
        "fox_w_qg": fox_w_qg.astype(BF16), "q_norm": row(jnp.tile(q_norm, (1, heads))),
        "fox_w_o": fox_w_o.astype(BF16),
        "ffn_w_in": ffn_w_in.astype(BF16), "ffn_conv_w": ffn_conv_w, "ffn_conv_b": row(ffn_conv_b),
        "ffn_w_out": ffn_w_out.astype(BF16),
        "hgrn_prompt_consts": _level_mats(CHUNK, CHUNK), "hgrn_decode_consts": _level_mats(CHUNK, ts),
        "bd_fox": _block_diag_ones(d, FOX_HEAD),
        "tri_prompt": _seg_lower_tri(ROW_TILE, ROW_TILE), "tri_decode": _seg_lower_tri(ROW_TILE, ts),
        "upper": _upper_tri(PAGE), "head_mask": _head_mask(heads, FOX_HEAD),
        "aug": _aug_consts(heads),
    }

    y_p, k_p, v_p, lf_p, hgrn_p, conv_p = _trunk(x_prompt.reshape(b * t, d), t, None, None, None, w)

    past = (cache_k.transpose(0, 2, 3, 1).reshape(n_phys, d, page), cache_v.transpose(0, 2, 3, 1).reshape(n_phys, d, page),
            cache_logf.astype(F32).transpose(0, 2, 1), page_table)
    y_s, k_s, v_s, lf_s, hgrn_s, conv_s = _trunk(x_sample.reshape(bd * ts, d), ts, state_hgrn, state_conv, past, w)

    hd = (heads, d // heads)
    return (y_p.reshape(b, t, d), y_s.reshape(bd, ts, d),
            k_p.reshape((b, t) + hd), v_p.reshape((b, t) + hd), lf_p.reshape(b, t, heads),
            k_s.reshape((bd, ts) + hd), v_s.reshape((bd, ts) + hd), lf_s.reshape(bd, ts, heads),
            hgrn_p, hgrn_s, conv_p, conv_s)
```

```python
import functools

import numpy as np
import jax
import jax.numpy as jnp
from jax import lax
from jax.experimental import pallas as pl
from jax.experimental.pallas import tpu as pltpu

F32 = jnp.float32
BF16 = jnp.bfloat16
EPS = 1e-6

LANES = 128
SUBLANES = 8
VMEM_LIMIT_BYTES = 56 * 1024 * 1024

HGRN_HEAD = 128
FOX_HEAD = 64
CONV_W = 3
CHUNK = 128
ROW_TILE = 256
ATTN_TILE = 512
PAGE = 128
DECODE_PAGES = 4
LOG2E = 1.4426950408889634


def _params(semantics):
    return pltpu.CompilerParams(dimension_semantics=semantics, vmem_limit_bytes=VMEM_LIMIT_BYTES)


def _const_spec(a):
    if isinstance(a, tuple):
        arr, layer = a
        return pl.BlockSpec((None,) + arr.shape[1:], lambda *_: (layer,) + (0,) * (arr.ndim - 1))
    return pl.BlockSpec(a.shape, lambda *_: (0,) * a.ndim)


def _arr(a):
    return a[0] if isinstance(a, tuple) else a


def _dot(a, b):
    return jnp.dot(a, b, preferred_element_type=F32)


def _dot_nt(a, b):
    return lax.dot_general(a, b, (((1,), (1,)), ((), ())), preferred_element_type=F32)


def _split3(x):
    hi = x.astype(BF16)
    r = x - hi.astype(F32)
    mid = r.astype(BF16)
    lo = (r - mid.astype(F32)).astype(BF16)
    return hi, mid, lo


def _dot3_right(t, x):
    hi, mid, lo = _split3(x)
    return _dot(t, hi) + _dot(t, mid) + _dot(t, lo)


def _dot3_left(x, t):
    hi, mid, lo = _split3(x)
    return _dot(hi, t) + _dot(mid, t) + _dot(lo, t)


def _rms(x, g):
    r = lax.rsqrt(jnp.mean(x * x, axis=-1, keepdims=True) + EPS)
    return x * r * g


def _sigmoid(x):
    return 1.0 / (1.0 + jnp.exp(-x))


def _log_sigmoid(x):
    return jnp.minimum(x, 0.0) - jnp.log(1.0 + jnp.exp(-jnp.abs(x)))


def _group_rms(x, bd_ref, width):
    ss = _dot((x * x).astype(BF16), bd_ref[...])
    return x * lax.rsqrt(ss * (1.0 / width) + EPS)


def _level_mats(rows, seg):
    t = np.arange(rows)[:, None]
    j = np.arange(rows)[None, :]
    mats = [((t // seg) == (j // seg)) & (j <= t)]
    lv = np.full((rows, rows), -1, np.int32)
    lv[np.arange(rows), np.arange(rows)] = 0
    h, level = seg // 2, 1
    while h >= 1:
        blk = t // (2 * h)
        m = blk * 2 * h + h
        in_blk = (j // (2 * h)) == blk
        upper = t >= m
        if h < SUBLANES:
            mats.append(in_blk & np.where(upper, (j >= m) & (j <= t), (j >= t + 1) & (j <= m - 1)))
        lv[in_blk & upper & ((j % (2 * h)) < h)] = level
        h //= 2
        level += 1
    tall = np.concatenate([m.astype(np.float32) for m in mats], axis=0)
    return jnp.asarray(tall, BF16), jnp.asarray(lv), seg


def _block_diag_ones(n, width):
    i = np.arange(n)
    return jnp.asarray((i[:, None] // width) == (i[None, :] // width), BF16)


def _seg_lower_tri(n, seg):
    i = np.arange(n)
    return jnp.asarray(((i[:, None] // seg) == (i[None, :] // seg)) & (i[None, :] <= i[:, None]), BF16)


def _upper_tri(n):
    i = np.arange(n)
    return jnp.asarray(i[:, None] <= i[None, :], BF16)


def _head_mask(heads, width):
    lane = np.arange(heads * width)
    return jnp.asarray((lane[None, :] // width) == np.arange(heads)[:, None], F32)


def _hgrn_in_kernel(x_ref, g_ref, w_ref, lbp_ref, q_ref, k_ref, lf_ref, i_ref, gt_ref, *, layer):
    d = q_ref.shape[1]
    hn = _rms(x_ref[...], g_ref[...]).astype(BF16)
    proj = _dot(hn, w_ref[...])
    q = proj[:, :d]
    fr = proj[:, d:2 * d]
    q_ref[...] = (q * _sigmoid(q)).astype(BF16)
    i_ref[...] = proj[:, 2 * d:3 * d].astype(BF16)
    gt_ref[...] = proj[:, 3 * d:].astype(BF16)
    if layer == 0:
        lf_ref[...] = _log_sigmoid(fr)
        k_ref[...] = _sigmoid(-fr).astype(BF16)
    else:
        lbp = lbp_ref[...]
        e = jnp.exp(lbp - jnp.max(lbp, axis=0, keepdims=True))
        p = e / jnp.sum(e, axis=0, keepdims=True)
        lb = jnp.sum(p[1:layer + 1], axis=0, keepdims=True)
        lf_ref[...] = jnp.log(lb + (1.0 - lb) * _sigmoid(fr))
        k_ref[...] = ((1.0 - lb) * _sigmoid(-fr)).astype(BF16)


def _hgrn_in(x, g, w, lbp, layer):
    n, d = x.shape
    tm = min(ROW_TILE, n)
    row = pl.BlockSpec((tm, d), lambda i: (i, 0))
    out = lambda dtype: jax.ShapeDtypeStruct((n, d), dtype)
    return pl.pallas_call(
        functools.partial(_hgrn_in_kernel, layer=layer),
        grid=(n // tm,),
        in_specs=[row, _const_spec(g), _const_spec(w), _const_spec(lbp)],
        out_specs=[row] * 5,
        out_shape=[out(BF16), out(BF16), out(F32), out(BF16), out(BF16)],
        compiler_params=_params(("arbitrary",)),
    )(x, _arr(g), _arr(w), lbp)


def _hgrn_intra(q, k, lf, v_bf, tall_ref, lv, seg):
    c, width = q.shape
    hi, mid, lo = _split3(lf)
    d3 = _dot(tall_ref[...], jnp.concatenate([hi, mid, lo], axis=1))
    d = d3[:, :c] + d3[:, c:2 * c] + d3[:, 2 * c:]
    b = d[:c]
    a = jnp.where(lv == 0, _dot_nt(q.astype(BF16), k.astype(BF16)), 0.0)
    h, level, blk = seg // 2, 1, 1
    while h >= 1:
        if h >= SUBLANES:
            bh = b.reshape(c // (2 * h), 2 * h, width)
            e = jnp.exp(-jnp.abs(bh - bh[:, h - 1:h, :])).reshape(c, width)
        else:
            e = jnp.exp(d[blk * c:(blk + 1) * c])
            blk += 1
        a = jnp.where(lv == level, _dot_nt((q * e).astype(BF16), (k * e).astype(BF16)), a)
        h //= 2
        level += 1
    bs = b.reshape(c // seg, seg, width)
    return _dot(a.astype(BF16), v_bf), b, (bs[:, seg - 1:seg, :] - bs).reshape(c, width)


def _hgrn_prompt_kernel(q_ref, k_ref, lf_ref, v_ref, tall_ref, lv_ref, o_ref, sout_ref, s_scr):
    j = pl.program_id(2)

    @pl.when(j == 0)
    def _():
        s_scr[...] = jnp.zeros_like(s_scr)

    lv = lv_ref[...]
    s = s_scr[...]
    for c in range(q_ref.shape[0] // CHUNK):
        rows = pl.ds(c * CHUNK, CHUNK)
        q, k, lf, v = q_ref[rows, :], k_ref[rows, :], lf_ref[rows, :], v_ref[rows, :]
        v_bf = v.astype(BF16)
        o_intra, b, rev = _hgrn_intra(q, k, lf, v_bf, tall_ref, lv, CHUNK)
        eb = jnp.exp(b)
        o_ref[rows, :] = (o_intra + _dot((q * eb).astype(BF16), s.astype(BF16))).astype(o_ref.dtype)
        ks_t = (k * jnp.exp(rev)).T.astype(BF16)
        decay = eb.T[:, CHUNK - 1:CHUNK]
        s = s * decay + _dot(ks_t, v_bf)
    s_scr[...] = s

    @pl.when(j == pl.num_programs(2) - 1)
    def _():
        sout_ref[0, 0] = s


def _hgrn_prompt(q, k, lf, v, batch, consts):
    tall, lv, _ = consts
    n, d = q.shape
    heads = d // HGRN_HEAD
    t = n // batch
    tb = min(4 * CHUNK, t)
    nj = t // tb
    blk = pl.BlockSpec((tb, HGRN_HEAD), lambda b, h, j: (b * nj + j, h))
    full = lambda a: pl.BlockSpec(a.shape, lambda b, h, j: (0,) * a.ndim)
    return pl.pallas_call(
        _hgrn_prompt_kernel,
        grid=(batch, heads, nj),
        in_specs=[blk, blk, blk, blk, full(tall), full(lv)],
        out_specs=[blk, pl.BlockSpec((1, 1, HGRN_HEAD, HGRN_HEAD), lambda b, h, j: (b, h, 0, 0))],
        out_shape=[jax.ShapeDtypeStruct((n, d), BF16),
                   jax.ShapeDtypeStruct((batch, heads, HGRN_HEAD, HGRN_HEAD), F32)],
        scratch_shapes=[pltpu.VMEM((HGRN_HEAD, HGRN_HEAD), F32)],
        compiler_params=_params(("arbitrary", "arbitrary", "arbitrary")),
    )(q, k, lf, v, tall, lv)


def _hgrn_decode_kernel(q_ref, k_ref, lf_ref, v_ref, s0_ref, tall_ref, lv_ref, o_ref, sout_ref, *, seq):
    q, k, lf, v = q_ref[...], k_ref[...], lf_ref[...], v_ref[...]
    v_bf = v.astype(BF16)
    o_intra, b, rev = _hgrn_intra(q, k, lf, v_bf, tall_ref, lv_ref[...], seq)
    eb = jnp.exp(b)
    qs = q * eb
    ks_t = (k * jnp.exp(rev)).T
    eb_t = eb.T
    lane = lax.broadcasted_iota(jnp.int32, ks_t.shape, 1)
    row = lax.broadcasted_iota(jnp.int32, qs.shape, 0)
    o = o_intra
    for i in range(CHUNK // seq):
        s0 = s0_ref[i, 0]
        mine = (lane >= i * seq) & (lane < (i + 1) * seq)
        o_i = _dot(qs.astype(BF16), s0.astype(BF16))
        o = o + jnp.where((row >= i * seq) & (row < (i + 1) * seq), o_i, 0.0)
        ds = _dot(jnp.where(mine, ks_t, 0.0).astype(BF16), v_bf)
        sout_ref[i, 0] = s0 * eb_t[:, (i + 1) * seq - 1:(i + 1) * seq] + ds
    o_ref[...] = o.astype(o_ref.dtype)


def _hgrn_decode(q, k, lf, v, s0_all, layer, seq, consts):
    tall, lv, _ = consts
    n, d = q.shape
    heads = d // HGRN_HEAD
    per = CHUNK // seq
    blk = pl.BlockSpec((CHUNK, HGRN_HEAD), lambda g, h: (g, h))
    sblk = pl.BlockSpec((per, 1, HGRN_HEAD, HGRN_HEAD), lambda g, h: (g, h, 0, 0))
    s0blk = pl.BlockSpec((None, per, 1, HGRN_HEAD, HGRN_HEAD), lambda g, h: (layer, g, h, 0, 0))
    return pl.pallas_call(
        functools.partial(_hgrn_decode_kernel, seq=seq),
        grid=(n // CHUNK, heads),
        in_specs=[blk, blk, blk, blk, s0blk, _const_spec(tall), _const_spec(lv)],
        out_specs=[blk, sblk],
        out_shape=[jax.ShapeDtypeStruct((n, d), BF16), jax.ShapeDtypeStruct(s0_all.shape[1:], F32)],
        compiler_params=_params(("arbitrary", "arbitrary")),
    )(q, k, lf, v, s0_all, tall, lv)


def _hgrn_out_kernel(o_ref, gt_ref, x_ref, on_ref, w_ref, h_ref):
    o = o_ref[...].astype(F32)
    parts = []
    for h in range(o.shape[1] // HGRN_HEAD):
        oh = o[:, h * HGRN_HEAD:(h + 1) * HGRN_HEAD]
        parts.append(oh * lax.rsqrt(jnp.mean(oh * oh, axis=-1, keepdims=True) + EPS))
    gt = gt_ref[...].astype(F32)
    y = jnp.concatenate(parts, axis=-1) * on_ref[...] * (gt * _sigmoid(gt))
    h_ref[...] = x_ref[...] + _dot(y.astype(BF16), w_ref[...])


def _fox_out_kernel(o_ref, gt_ref, x_ref, w_ref, h_ref):
    y = o_ref[...].astype(F32) * _sigmoid(gt_ref[...].astype(F32))
    h_ref[...] = x_ref[...] + _dot(y.astype(BF16), w_ref[...])


def _mix_out(kernel, o, gt, x, *consts):
    n, d = x.shape
    tm = min(ROW_TILE, n)
    row = pl.BlockSpec((tm, d), lambda i: (i, 0))
    return pl.pallas_call(
        kernel,
        grid=(n // tm,),
        in_specs=[row, row, row] + [_const_spec(c) for c in consts],
        out_specs=row,
        out_shape=jax.ShapeDtypeStruct((n, d), F32),
        compiler_params=_params(("arbitrary",)),
    )(o, gt, x, *[_arr(c) for c in consts])


def _ffn_tail(h, u, u1, u2, gate, cw_ref, cb_ref, wout_ref):
    a = cb_ref[...] + cw_ref[0:1, :] * u2 + cw_ref[1:2, :] * u1 + cw_ref[2:3, :] * u
    y = a * _sigmoid(a) * gate
    return h + _dot(y.astype(BF16), wout_ref[...])


def _ffn_prompt_kernel(x_ref, g_ref, win_ref, cw_ref, cb_ref, wout_ref, h_ref, cs_ref, carry):
    j = pl.program_id(1)
    f = cw_ref.shape[1]

    @pl.when(j == 0)
    def _():
        carry[...] = jnp.zeros_like(carry)

    h = x_ref[...]
    tm = h.shape[0]
    ug = _dot(_rms(h, g_ref[...]).astype(BF16), win_ref[...])
    u = ug[:, :f]
    prev = carry[...]
    p0 = prev[SUBLANES - 2:SUBLANES - 1, :]
    p1 = prev[SUBLANES - 1:SUBLANES, :]
    row = lax.broadcasted_iota(jnp.int32, u.shape, 0)
    u1 = jnp.where(row == 0, p1, pltpu.roll(u, 1, axis=0))
    u2 = jnp.where(row == 0, p0, jnp.where(row == 1, p1, pltpu.roll(u, 2, axis=0)))
    h_ref[...] = _ffn_tail(h, u, u1, u2, ug[:, f:], cw_ref, cb_ref, wout_ref)
    carry[...] = u[tm - SUBLANES:, :]

    @pl.when(j == pl.num_programs(1) - 1)
    def _():
        cs_ref[0] = u[tm - (CONV_W - 1):, :]


def _ffn_prompt(x, g, win, cw, cb, wout, batch):
    n, d = x.shape
    f = _arr(cw).shape[-1]
    t = n // batch
    tm = min(ROW_TILE, t)
    nj = t // tm
    row = pl.BlockSpec((tm, d), lambda b, j: (b * nj + j, 0))
    consts = [g, win, cw, cb, wout]
    return pl.pallas_call(
        _ffn_prompt_kernel,
        grid=(batch, nj),
        in_specs=[row] + [_const_spec(c) for c in consts],
        out_specs=[row, pl.BlockSpec((1, CONV_W - 1, f), lambda b, j: (b, 0, 0))],
        out_shape=[jax.ShapeDtypeStruct((n, d), F32), jax.ShapeDtypeStruct((batch, CONV_W - 1, f), F32)],
        scratch_shapes=[pltpu.VMEM((SUBLANES, f), F32)],
        compiler_params=_params(("arbitrary", "arbitrary")),
    )(x, *[_arr(c) for c in consts])


def _ffn_decode_kernel(x_ref, prev_ref, g_ref, win_ref, cw_ref, cb_ref, wout_ref, h_ref, cs_ref, *, seq):
    f = cw_ref.shape[1]
    h = x_ref[...]
    tm = h.shape[0]
    ug = _dot(_rms(h, g_ref[...]).astype(BF16), win_ref[...])
    u = ug[:, :f]
    u3 = u.reshape(tm // seq, seq, f)
    prev = prev_ref[...]
    p0 = prev[:, 0:1, :]
    p1 = prev[:, 1:2, :]
    row = lax.broadcasted_iota(jnp.int32, u3.shape, 1)
    u1 = jnp.where(row == 0, p1, pltpu.roll(u3, 1, axis=1))
    u2 = jnp.where(row == 0, p0, jnp.where(row == 1, p1, pltpu.roll(u3, 2, axis=1)))
    h_ref[...] = _ffn_tail(h, u, u1.reshape(tm, f), u2.reshape(tm, f), ug[:, f:], cw_ref, cb_ref, wout_ref)
    cs_ref[...] = u3[:, seq - (CONV_W - 1):, :]


def _ffn_decode(x, prev_all, layer, g, win, cw, cb, wout, seq):
    n, d = x.shape
    f = _arr(cw).shape[-1]
    tm = min(ROW_TILE, n)
    row = pl.BlockSpec((tm, d), lambda i: (i, 0))
    st = pl.BlockSpec((tm // seq, CONV_W - 1, f), lambda i: (i, 0, 0))
    st_in = pl.BlockSpec((None, tm // seq, CONV_W - 1, f), lambda i: (layer, i, 0, 0))
    consts = [g, win, cw, cb, wout]
    return pl.pallas_call(
        functools.partial(_ffn_decode_kernel, seq=seq),
        grid=(n // tm,),
        in_specs=[row, st_in] + [_const_spec(c) for c in consts],
        out_specs=[row, st],
        out_shape=[jax.ShapeDtypeStruct((n, d), F32), jax.ShapeDtypeStruct(prev_all.shape[1:], F32)],
        compiler_params=_params(("arbitrary",)),
    )(x, prev_all, *[_arr(c) for c in consts])


def _aug_consts(heads):
    width = heads * LANES
    pq, pk = np.zeros((3 * LANES, width), np.float32), np.zeros((3 * LANES, width), np.float32)
    oq, ok, real = (np.zeros((1, width), np.float32) for _ in range(3))
    for h in range(heads):
        e = h % 2
        free = LANES * h + FOX_HEAD * (1 - e)
        real[0, LANES * h + FOX_HEAD * e:LANES * h + FOX_HEAD * (e + 1)] = 1.0
        for x in range(3):
            pq[LANES * x + h, free + x] = 1.0
            pk[LANES * x + h, free + 3 + x] = -1.0
            oq[0, free + 3 + x] = 1.0
            ok[0, free + x] = 1.0
    return (jnp.asarray(real), (jnp.asarray(pq, BF16), jnp.asarray(oq)), (jnp.asarray(pk, BF16), jnp.asarray(ok)))


def _head_slots(x, real, bias):
    pairs = x.shape[1] // LANES
    dup = jnp.concatenate([x[:, LANES * p:LANES * (p + 1)] for p in range(pairs) for _ in range(2)], axis=1)
    return jnp.where(real > 0.0, dup, bias).astype(BF16)


def _bias_slots(c, p_ref, ones_ref):
    hi, mid, lo = _split3(c * LOG2E)
    return _dot(jnp.concatenate([hi, mid, lo], axis=1), p_ref[...]) + ones_ref[...]


def _shared_kv_kernel(x_ref, g_ref, wk_ref, wv_ref, wf_ref, bf_ref, kn_ref, bd_ref, tri_ref, init_ref, *rest, aug):
    if aug:
        real_ref, pk_ref, ok_ref, k_ref, v_ref, lf_ref, c_ref, ka_ref, va_ref, carry = rest
    else:
        k_ref, v_ref, lf_ref, c_ref, carry = rest
    j = pl.program_id(1)
    heads = lf_ref.shape[1]

    @pl.when(j == 0)
    def _():
        carry[...] = jnp.zeros_like(carry)

    hn = _rms(x_ref[...], g_ref[...]).astype(BF16)
    k = _group_rms(_dot(hn, wk_ref[...]), bd_ref, FOX_HEAD) * kn_ref[...]
    v = _dot(hn, wv_ref[...])
    k_ref[...] = k
    v_ref[...] = v
    lf = _log_sigmoid(_dot(hn, wf_ref[...]) + bf_ref[...])
    c = _dot3_right(tri_ref[...], lf) + carry[...] + init_ref[...]
    carry[...] = c[c.shape[0] - 1:, :] - init_ref[c.shape[0] - 1:, :]
    lf_ref[...] = lf[:, :heads]
    c_ref[...] = c
    if aug:
        real = real_ref[...]
        ka_ref[...] = _head_slots(k, real, _bias_slots(c, pk_ref, ok_ref))
        va_ref[...] = _head_slots(v, real, 1.0)


def _shared_kv(x, g, wk, wv, wf, bf, kn, bd, tri, init, groups, heads, aug=None):
    n, d = x.shape
    tm = tri.shape[0]
    nj = n // groups // tm
    row = lambda w: pl.BlockSpec((tm, w), lambda b, j: (b * nj + j, 0))
    full = lambda a: pl.BlockSpec(a.shape, lambda b, j: (0,) * a.ndim)
    consts = [g, wk, wv, wf, bf, kn, bd, tri]
    out_specs = [row(d), row(d), row(heads), row(LANES)]
    out_shape = [jax.ShapeDtypeStruct((n, d), F32), jax.ShapeDtypeStruct((n, d), F32),
                 jax.ShapeDtypeStruct((n, heads), F32), jax.ShapeDtypeStruct((n, LANES), F32)]
    extra = []
    if aug is not None:
        real, _, (pk, ok) = aug
        extra = [real, pk, ok]
        out_specs += [row(heads * LANES)] * 2
        out_shape += [jax.ShapeDtypeStruct((n, heads * LANES), BF16)] * 2
    return pl.pallas_call(
        functools.partial(_shared_kv_kernel, aug=aug is not None),
        grid=(groups, nj),
        in_specs=[row(d)] + [full(a) for a in consts] + [row(LANES)] + [full(a) for a in extra],
        out_specs=out_specs,
        out_shape=out_shape,
        scratch_shapes=[pltpu.VMEM((1, LANES), F32)],
        compiler_params=_params(("arbitrary", "arbitrary")),
    )(x, *consts, init, *extra)


def _fox_qg_kernel(x_ref, g_ref, w_ref, qn_ref, bd_ref, *rest, aug):
    d = x_ref.shape[1]
    qg = _dot(_rms(x_ref[...], g_ref[...]).astype(BF16), w_ref[...])
    q = _group_rms(qg[:, :d], bd_ref, FOX_HEAD) * (qn_ref[...] * FOX_HEAD ** -0.5)
    if aug:
        c_ref, real_ref, pq_ref, oq_ref, q_ref, gt_ref = rest
        q_ref[...] = _head_slots(q * LOG2E, real_ref[...], _bias_slots(c_ref[...], pq_ref, oq_ref))
    else:
        q_ref, gt_ref = rest
        q_ref[...] = q
    gt_ref[...] = qg[:, d:].astype(BF16)


def _fox_qg(x, g, w, qn, bd, c=None, aug=None):
    n, d = x.shape
    tm = min(ROW_TILE, n)
    row = lambda w_: pl.BlockSpec((tm, w_), lambda i: (i, 0))
    full = lambda a: pl.BlockSpec(a.shape, lambda i: (0,) * a.ndim)
    consts = [g, w, qn, bd]
    if aug is None:
        extra, extra_specs = [], []
        q_spec, q_shape = row(d), jax.ShapeDtypeStruct((n, d), F32)
    else:
        real, (pq, oq), _ = aug
        extra, extra_specs = [c, real, pq, oq], [row(LANES), full(real), full(pq), full(oq)]
        width = real.shape[1]
        q_spec, q_shape = row(width), jax.ShapeDtypeStruct((n, width), BF16)
    return pl.pallas_call(
        functools.partial(_fox_qg_kernel, aug=aug is not None),
        grid=(n // tm,),
        in_specs=[row(d)] + [_const_spec(a) for a in consts] + extra_specs,
        out_specs=[q_spec, row(d)],
        out_shape=[q_shape, jax.ShapeDtypeStruct((n, d), BF16)],
        compiler_params=_params(("arbitrary",)),
    )(x, *[_arr(a) for a in consts], *extra)


def _attn_prompt_kernel(q_ref, k_ref, v_ref, o_ref):
    i = pl.program_id(2)
    tq = q_ref.shape[0]

    def slot(ref, rows, e):
        return ref[rows, e * LANES:(e + 1) * LANES]

    def block(j, state, keep):
        rows = pl.ds(pl.multiple_of(j * tq, tq), tq)
        out = []
        for e in range(2):
            m, acc = state[2 * e:2 * e + 2]
            s = _dot_nt(slot(q_ref, slice(None), e), slot(k_ref, rows, e))
            if keep is not None:
                s = jnp.where(keep, s, -jnp.inf)
            m_new = jnp.maximum(m, jnp.max(s, axis=-1, keepdims=True))
            p = jnp.exp2(s - m_new)
            out += [m_new, jnp.exp2(m - m_new) * acc + _dot(p.astype(BF16), slot(v_ref, rows, e))]
        return tuple(out)

    init = (jnp.full((tq, 1), -jnp.inf, F32), jnp.zeros((tq, LANES), F32)) * 2
    state = lax.fori_loop(0, i, lambda j, st: block(j, st, None), init)
    causal = lax.broadcasted_iota(jnp.int32, (tq, tq), 1) <= lax.broadcasted_iota(jnp.int32, (tq, tq), 0)
    state = block(i, state, causal)
    outs = [acc / pltpu.roll(acc, FOX_HEAD, axis=1) for acc in (state[1], state[3])]
    lane = lax.broadcasted_iota(jnp.int32, (tq, LANES), 1)
    o_ref[...] = jnp.where(lane < FOX_HEAD, outs[0], outs[1]).astype(o_ref.dtype)


def _attn_prompt(qa, ka, va, batch):
    n, width = qa.shape
    t = n // batch
    tq = min(ATTN_TILE, t)
    nq = t // tq
    pairs = width // (2 * LANES)
    return pl.pallas_call(
        _attn_prompt_kernel,
        grid=(batch, pairs, nq),
        in_specs=[pl.BlockSpec((tq, 2 * LANES), lambda b, p, i: (b * nq + i, p)),
                  pl.BlockSpec((t, 2 * LANES), lambda b, p, i: (b, p)),
                  pl.BlockSpec((t, 2 * LANES), lambda b, p, i: (b, p))],
        out_specs=pl.BlockSpec((tq, LANES), lambda b, p, i: (b * nq + i, p)),
        out_shape=jax.ShapeDtypeStruct((n, pairs * LANES), BF16),
        compiler_params=_params(("arbitrary", "arbitrary", "arbitrary")),
    )(qa, ka, va)


def _paged_cumsum_kernel(pt_ref, *refs):
    del pt_ref
    n_pages = (len(refs) - 3)
    u_ref, c_ref, tot_ref = refs[n_pages:]
    carry = jnp.zeros(refs[0].shape[1:], F32)
    for p in range(n_pages):
        c = _dot3_left(refs[p][0], u_ref[...]) + carry
        c_ref[0, :, p * PAGE:(p + 1) * PAGE] = c
        carry = jnp.broadcast_to(c[:, PAGE - 1:PAGE], c.shape)
    tot_ref[0] = carry


def _paged_cumsum(logf_t, page_table, upper):
    bd, n_pages = page_table.shape
    heads = logf_t.shape[1]
    page = lambda p: pl.BlockSpec((1, heads, PAGE), lambda s, pt: (pt[s * n_pages + p], 0, 0))
    grid_spec = pltpu.PrefetchScalarGridSpec(
        num_scalar_prefetch=1,
        grid=(bd,),
        in_specs=[page(p) for p in range(n_pages)] + [pl.BlockSpec(upper.shape, lambda s, pt: (0, 0))],
        out_specs=[pl.BlockSpec((1, heads, n_pages * PAGE), lambda s, pt: (s, 0, 0)),
                   pl.BlockSpec((1, heads, PAGE), lambda s, pt: (s, 0, 0))],
    )
    return pl.pallas_call(
        _paged_cumsum_kernel,
        grid_spec=grid_spec,
        out_shape=[jax.ShapeDtypeStruct((bd, heads, n_pages * PAGE), F32),
                   jax.ShapeDtypeStruct((bd, heads, PAGE), F32)],
        compiler_params=_params(("arbitrary",)),
    )(page_table.reshape(-1), *([logf_t] * n_pages), upper)


def _attn_decode_kernel(pt_ref, q_ref, *refs, n_pages):
    del pt_ref
    kp, vp = refs[:n_pages], refs[n_pages:2 * n_pages]
    kn_ref, vn_ref, cp_ref, cn_ref, cq_ref, hm_ref, o_ref = refs[2 * n_pages:]
    seq, d = q_ref.shape[1:]
    hm = hm_ref[...]
    heads = hm.shape[0]
    rows = seq * heads
    q = (q_ref[0][:, None, :] * hm[None, :, :]).reshape(rows, d).astype(BF16)
    cq = cq_ref[0]

    def update(state, s, ck, keep, pv):
        m, l, acc = state
        s = s + (cq - jnp.concatenate([ck] * seq, axis=0))
        if keep is not None:
            s = jnp.where(keep, s, -jnp.inf)
        m_new = jnp.maximum(m, jnp.max(s, axis=-1, keepdims=True))
        pr = jnp.exp(s - m_new)
        alpha = jnp.exp(m - m_new)
        return m_new, alpha * l + jnp.sum(pr, axis=-1, keepdims=True), alpha * acc + pv(pr.astype(BF16))

    state = (jnp.full((rows, 1), -jnp.inf, F32), jnp.zeros((rows, 1), F32), jnp.zeros((rows, d), F32))
    for g0 in range(0, n_pages, DECODE_PAGES):
        group = range(g0, g0 + DECODE_PAGES)
        k_cat = jnp.concatenate([kp[g][0].astype(BF16) for g in group], axis=1)
        v_cat = jnp.concatenate([vp[g][0].astype(BF16) for g in group], axis=1)
        ck = cp_ref[0, :, g0 * PAGE:(g0 + DECODE_PAGES) * PAGE]
        state = update(state, _dot(q, k_cat), ck, None, lambda pr, v_cat=v_cat: _dot_nt(pr, v_cat))

    pad = jnp.zeros((PAGE - seq, d), F32)
    k_new = jnp.concatenate([kn_ref[0], pad], axis=0).astype(BF16)
    v_new = jnp.concatenate([vn_ref[0], pad], axis=0).astype(BF16)
    t = lax.broadcasted_iota(jnp.int32, (rows, PAGE), 0) // heads
    key = lax.broadcasted_iota(jnp.int32, (rows, PAGE), 1)
    _, l, acc = update(state, _dot_nt(q, k_new), cn_ref[0], key <= t, lambda pr: _dot(pr, v_new))
    o_ref[0] = jnp.sum((acc / l).reshape(seq, heads, d) * hm[None, :, :], axis=1)


def _attn_decode(q, cache_kt, cache_vt, k_new, v_new, c_past_t, c_new_t, cq_col, head_mask, page_table):
    bd, seq, d = q.shape
    n_pages = page_table.shape[1]
    heads = head_mask.shape[0]
    page = lambda g: pl.BlockSpec((1, d, PAGE), lambda s, pt: (pt[s * n_pages + g], 0, 0))
    tok = pl.BlockSpec((1, seq, d), lambda s, pt: (s, 0, 0))
    grid_spec = pltpu.PrefetchScalarGridSpec(
        num_scalar_prefetch=1,
        grid=(bd,),
        in_specs=[tok] + [page(g) for g in range(n_pages)] + [page(g) for g in range(n_pages)] + [
            tok, tok,
            pl.BlockSpec((1, heads, n_pages * PAGE), lambda s, pt: (s, 0, 0)),
            pl.BlockSpec((1, heads, PAGE), lambda s, pt: (s, 0, 0)),
            pl.BlockSpec((1, seq * heads, 1), lambda s, pt: (s, 0, 0)),
            pl.BlockSpec(head_mask.shape, lambda s, pt: (0, 0))],
        out_specs=tok,
    )
    return pl.pallas_call(
        functools.partial(_attn_decode_kernel, n_pages=n_pages),
        grid_spec=grid_spec,
        out_shape=jax.ShapeDtypeStruct((bd, seq, d), F32),
        compiler_params=_params(("arbitrary",)),
    )(page_table.reshape(-1), q, *([cache_kt] * n_pages), *([cache_vt] * n_pages), k_new, v_new, c_past_t,
      c_new_t, cq_col, head_mask)


def _trunk(x, seq, hgrn_s0, conv_s0, past, w):
    n, d = x.shape
    n_seq = n // seq
    depth = w["ffn_w_in"].shape[0]
    n_a = w["hgrn_w_in"].shape[0]
    heads = w["fox_b_f"].shape[0]
    prompt = past is None
    h = x
    new_hgrn, new_conv = [], []
    for layer in range(depth):
        if layer < n_a:
            q, k, lf, i, gt = _hgrn_in(h, (w["norm_mix"], layer), (w["hgrn_w_in"], layer), w["hgrn_lb_param"], layer)
            if prompt:
                o, s = _hgrn_prompt(q, k, lf, i, n_seq, w["hgrn_prompt_consts"])
            else:
                o, s = _hgrn_decode(q, k, lf, i, hgrn_s0, layer, seq, w["hgrn_decode_consts"])
            new_hgrn.append(s)
            h = _mix_out(_hgrn_out_kernel, o, gt, h, (w["hgrn_onorm"], layer), (w["hgrn_w_o"], layer))
        else:
            j = layer - n_a
            qg = (h, (w["norm_mix"], layer), (w["fox_w_qg"], j), (w["q_norm"], j), w["bd_fox"])
            if prompt:
                qa, gt = _fox_qg(*qg, c, w["aug"])
                o = _attn_prompt(qa, ka, va, n_seq)
            else:
                q, gt = _fox_qg(*qg)
                o = _attn_decode(q.reshape(n_seq, seq, d), past[0], past[1], k_sh.reshape(n_seq, seq, d),
                                 v_sh.reshape(n_seq, seq, d), c_past_t, c_new_t, cq_col, w["head_mask"],
                                 past[3]).reshape(n, d)
            h = _mix_out(_fox_out_kernel, o, gt, h, (w["fox_w_o"], j))
        ffn = [(w[name], layer) for name in ("norm_ffn", "ffn_w_in", "ffn_conv_w", "ffn_conv_b", "ffn_w_out")]
        if prompt:
            h, cs = _ffn_prompt(h, *ffn, n_seq)
        else:
            h, cs = _ffn_decode(h, conv_s0, layer, *ffn, seq)
        new_conv.append(cs)
        if layer == n_a - 1:
            if prompt:
                init = jnp.zeros((n, LANES), F32)
                tri, groups = w["tri_prompt"], n_seq
            else:
                c_past_t, total = _paged_cumsum(past[2], past[3], w["upper"])
                init = jnp.pad(jnp.repeat(total[:, :, 0], seq, axis=0), ((0, 0), (0, LANES - heads)))
                tri, groups = w["tri_decode"], n // w["tri_decode"].shape[0]
            kv = _shared_kv(h, w["kv_norm"], w["w_k"], w["w_v"], w["w_f"], w["b_f"], w["k_norm"], w["bd_fox"], tri,
                            init, groups, heads, w["aug"] if prompt else None)
            k_sh, v_sh, logf, c = kv[:4]
            if prompt:
                ka, va = kv[4:]
            else:
                c3 = c[:, :heads].reshape(n_seq, seq, heads)
                cq_col = c3.reshape(n_seq, seq * heads, 1)
                c_new_t = jnp.pad(c3.transpose(0, 2, 1), ((0, 0), (0, 0), (0, PAGE - seq)))
    return h, k_sh, v_sh, logf, jnp.stack(new_hgrn), jnp.stack(new_conv)


def kernel(x_prompt, x_sample, cache_k, cache_v, cache_logf, state_hgrn, state_conv, page_table, norm_mix, norm_ffn, hgrn_w_in, hgrn_lb_param, hgrn_onorm, hgrn_w_o, kv_norm, w_kvf, fox_b_f, k_norm, fox_w_qg, q_norm, fox_w_o, ffn_w_in, ffn_conv_w, ffn_conv_b, ffn_w_out):
    b, t, d = x_prompt.shape
    bd, ts, _ = x_sample.shape
    heads = fox_b_f.shape[0]
    n_phys, page = cache_k.shape[:2]
    assert page == PAGE and d % LANES == 0 and d // heads == FOX_HEAD
    assert t % ATTN_TILE == 0 and page_table.shape[1] % DECODE_PAGES == 0
    assert t % ROW_TILE == 0 and (bd * ts) % ROW_TILE == 0 and CHUNK % ts == 0 and ROW_TILE % ts == 0
    assert ts >= CONV_W - 1

    row = lambda a: a.reshape(a.shape[:-1] + (1, a.shape[-1]))
    w = {
        "norm_mix": row(norm_mix), "norm_ffn": row(norm_ffn),
        "hgrn_w_in": hgrn_w_in.astype(BF16), "hgrn_lb_param": hgrn_lb_param,
        "hgrn_onorm": row(jnp.tile(hgrn_onorm, (1, d // HGRN_HEAD))), "hgrn_w_o": hgrn_w_o.astype(BF16),
        "kv_norm": row(kv_norm),
        "w_k": w_kvf[:, :d].astype(BF16), "w_v": w_kvf[:, d:2 * d].astype(BF16),
        "w_f": jnp.pad(w_kvf[:, 2 * d:], ((0, 0), (0, LANES - heads))).astype(BF16),
        "b_f": jnp.pad(fox_b_f, (0, LANES - heads)).reshape(1, LANES), "fox_b_f": fox_b_f,
        "k_norm": jnp.tile(k_norm, heads).reshape(1, d),
        "fox_w_qg": fox_w_qg.astype(BF16), "q_norm": row(jnp.tile(q_norm, (1, heads))),
        "fox_w_o": fox_w_o.astype(BF16),
        "ffn_w_in": ffn_w_in.astype(BF16), "ffn_conv_w": ffn_conv_w, "ffn_conv_b": row(ffn_conv_b),
        "ffn_w_out": ffn_w_out.astype(BF16),
        "hgrn_prompt_consts": _level_mats(CHUNK, CHUNK), "hgrn_decode_consts": _level_mats(CHUNK, ts),
        "bd_fox": _block_diag_ones(d, FOX_HEAD),
        "tri_prompt": _seg_lower_tri(ROW_TILE, ROW_TILE), "tri_decode": _seg_lower_tri(ROW_TILE, ts),
        "upper": _upper_tri(PAGE), "head_mask": _head_mask(heads, FOX_HEAD),
        "aug": _aug_consts(heads),
    }

    y_p, k_p, v_p, lf_p, hgrn_p, conv_p = _trunk(x_prompt.reshape(b * t, d), t, None, None, None, w)

    past = (cache_k.transpose(0, 2, 3, 1).reshape(n_phys, d, page), cache_v.transpose(0, 2, 3, 1).reshape(n_phys, d, page),
            cache_logf.astype(F32).transpose(0, 2, 1), page_table)
    y_s, k_s, v_s, lf_s, hgrn_s, conv_s = _trunk(x_sample.reshape(bd * ts, d), ts, state_hgrn, state_conv, past, w)

    hd = (heads, d // heads)
    return (y_p.reshape(b, t, d), y_s.reshape(bd, ts, d),
            k_p.reshape((b, t) + hd), v_p.reshape((b, t) + hd), lf_p.reshape(b, t, heads),
            k_s.reshape((bd, ts) + hd), v_s.reshape((bd, ts) + hd), lf_s.reshape(bd, ts, heads),
            hgrn_p, hgrn_s, conv_p, conv_s)
```

```python
import functools

import numpy as np
import jax
import jax.numpy as jnp
from jax import lax
from jax.experimental import pallas as pl
from jax.experimental.pallas import tpu as pltpu

F32 = jnp.float32
BF16 = jnp.bfloat16
EPS = 1e-6

LANES = 128
SUBLANES = 8
VMEM_LIMIT_BYTES = 56 * 1024 * 1024

HGRN_HEAD = 128
FOX_HEAD = 64
CONV_W = 3
CHUNK = 128
ROW_TILE = 256
ATTN_TILE = 512
PAGE = 128
DECODE_PAGES = 4
LOG2E = 1.4426950408889634


def _params(semantics):
    return pltpu.CompilerParams(dimension_semantics=semantics, vmem_limit_bytes=VMEM_LIMIT_BYTES)


def _const_spec(a):
    if isinstance(a, tuple):
        arr, layer = a
        return pl.BlockSpec((None,) + arr.shape[1:], lambda *_: (layer,) + (0,) * (arr.ndim - 1))
    return pl.BlockSpec(a.shape, lambda *_: (0,) * a.ndim)


def _arr(a):
    return a[0] if isinstance(a, tuple) else a


def _dot(a, b):
    return jnp.dot(a, b, preferred_element_type=F32)


def _dot_nt(a, b):
    return lax.dot_general(a, b, (((1,), (1,)), ((), ())), preferred_element_type=F32)


def _split3(x):
    hi = x.astype(BF16)
    r = x - hi.astype(F32)
    mid = r.astype(BF16)
    lo = (r - mid.astype(F32)).astype(BF16)
    return hi, mid, lo


def _dot3_right(t, x):
    hi, mid, lo = _split3(x)
    return _dot(t, hi) + _dot(t, mid) + _dot(t, lo)


def _dot3_left(x, t):
    hi, mid, lo = _split3(x)
    return _dot(hi, t) + _dot(mid, t) + _dot(lo, t)


def _rms(x, g):
    r = lax.rsqrt(jnp.mean(x * x, axis=-1, keepdims=True) + EPS)
    return x * r * g


def _sigmoid(x):
    return 1.0 / (1.0 + jnp.exp(-x))


def _log_sigmoid(x):
    return jnp.minimum(x, 0.0) - jnp.log(1.0 + jnp.exp(-jnp.abs(x)))


def _group_rms(x, bd_ref, width):
    ss = _dot((x * x).astype(BF16), bd_ref[...])
    return x * lax.rsqrt(ss * (1.0 / width) + EPS)


def _level_mats(rows, seg):
    t = np.arange(rows)[:, None]
    j = np.arange(rows)[None, :]
    mats = [((t // seg) == (j // seg)) & (j <= t)]
    lv = np.full((rows, rows), -1, np.int32)
    lv[np.arange(rows), np.arange(rows)] = 0
    h, level = seg // 2, 1
    while h >= 1:
        blk = t // (2 * h)
        m = blk * 2 * h + h
        in_blk = (j // (2 * h)) == blk
        upper = t >= m
        if h < SUBLANES:
            mats.append(in_blk & np.where(upper, (j >= m) & (j <= t), (j >= t + 1) & (j <= m - 1)))
        lv[in_blk & upper & ((j % (2 * h)) < h)] = level
        h //= 2
        level += 1
    tall = np.concatenate([m.astype(np.float32) for m in mats], axis=0)
    return jnp.asarray(tall, BF16), jnp.asarray(lv), seg


def _block_diag_ones(n, width):
    i = np.arange(n)
    return jnp.asarray((i[:, None] // width) == (i[None, :] // width), BF16)


def _seg_lower_tri(n, seg):
    i = np.arange(n)
    return jnp.asarray(((i[:, None] // seg) == (i[None, :] // seg)) & (i[None, :] <= i[:, None]), BF16)


def _upper_tri(n):
    i = np.arange(n)
    return jnp.asarray(i[:, None] <= i[None, :], BF16)


def _head_mask(heads, width):
    lane = np.arange(heads * width)
    return jnp.asarray((lane[None, :] // width) == np.arange(heads)[:, None], F32)


def _hgrn_in_kernel(x_ref, g_ref, w_ref, lbp_ref, q_ref, k_ref, lf_ref, i_ref, gt_ref, *, layer):
    d = q_ref.shape[1]
    hn = _rms(x_ref[...], g_ref[...]).astype(BF16)
    proj = _dot(hn, w_ref[...])
    q = proj[:, :d]
    fr = proj[:, d:2 * d]
    q_ref[...] = (q * _sigmoid(q)).astype(BF16)
    i_ref[...] = proj[:, 2 * d:3 * d].astype(BF16)
    gt_ref[...] = proj[:, 3 * d:].astype(BF16)
    if layer == 0:
        lf_ref[...] = _log_sigmoid(fr)
        k_ref[...] = _sigmoid(-fr).astype(BF16)
    else:
        lbp = lbp_ref[...]
        e = jnp.exp(lbp - jnp.max(lbp, axis=0, keepdims=True))
        p = e / jnp.sum(e, axis=0, keepdims=True)
        lb = jnp.sum(p[1:layer + 1], axis=0, keepdims=True)
        lf_ref[...] = jnp.log(lb + (1.0 - lb) * _sigmoid(fr))
        k_ref[...] = ((1.0 - lb) * _sigmoid(-fr)).astype(BF16)


def _hgrn_in(x, g, w, lbp, layer):
    n, d = x.shape
    tm = min(ROW_TILE, n)
    row = pl.BlockSpec((tm, d), lambda i: (i, 0))
    out = lambda dtype: jax.ShapeDtypeStruct((n, d), dtype)
    return pl.pallas_call(
        functools.partial(_hgrn_in_kernel, layer=layer),
        grid=(n // tm,),
        in_specs=[row, _const_spec(g), _const_spec(w), _const_spec(lbp)],
        out_specs=[row] * 5,
        out_shape=[out(BF16), out(BF16), out(F32), out(BF16), out(BF16)],
        compiler_params=_params(("arbitrary",)),
    )(x, _arr(g), _arr(w), lbp)


def _hgrn_intra(q, k, lf, v_bf, tall_ref, lv, seg):
    c, width = q.shape
    hi, mid, lo = _split3(lf)
    d3 = _dot(tall_ref[...], jnp.concatenate([hi, mid, lo], axis=1))
    d = d3[:, :c] + d3[:, c:2 * c] + d3[:, 2 * c:]
    b = d[:c]
    a = jnp.where(lv == 0, _dot_nt(q.astype(BF16), k.astype(BF16)), 0.0)
    h, level, blk = seg // 2, 1, 1
    while h >= 1:
        if h >= SUBLANES:
            bh = b.reshape(c // (2 * h), 2 * h, width)
            e = jnp.exp(-jnp.abs(bh - bh[:, h - 1:h, :])).reshape(c, width)
        else:
            e = jnp.exp(d[blk * c:(blk + 1) * c])
            blk += 1
        a = jnp.where(lv == level, _dot_nt((q * e).astype(BF16), (k * e).astype(BF16)), a)
        h //= 2
        level += 1
    bs = b.reshape(c // seg, seg, width)
    return _dot(a.astype(BF16), v_bf), b, (bs[:, seg - 1:seg, :] - bs).reshape(c, width)


def _hgrn_prompt_kernel(q_ref, k_ref, lf_ref, v_ref, tall_ref, lv_ref, o_ref, sout_ref, s_scr):
    j = pl.program_id(2)

    @pl.when(j == 0)
    def _():
        s_scr[...] = jnp.zeros_like(s_scr)

    lv = lv_ref[...]
    s = s_scr[...]
    for c in range(q_ref.shape[0] // CHUNK):
        rows = pl.ds(c * CHUNK, CHUNK)
        q, k, lf, v = q_ref[rows, :], k_ref[rows, :], lf_ref[rows, :], v_ref[rows, :]
        v_bf = v.astype(BF16)
        o_intra, b, rev = _hgrn_intra(q, k, lf, v_bf, tall_ref, lv, CHUNK)
        eb = jnp.exp(b)
        o_ref[rows, :] = (o_intra + _dot((q * eb).astype(BF16), s.astype(BF16))).astype(o_ref.dtype)
        ks_t = (k * jnp.exp(rev)).T.astype(BF16)
        decay = eb.T[:, CHUNK - 1:CHUNK]
        s = s * decay + _dot(ks_t, v_bf)
    s_scr[...] = s

    @pl.when(j == pl.num_programs(2) - 1)
    def _():
        sout_ref[0, 0] = s


def _hgrn_prompt(q, k, lf, v, batch, consts):
    tall, lv, _ = consts
    n, d = q.shape
    heads = d // HGRN_HEAD
    t = n // batch
    tb = min(8 * CHUNK, t)
    nj = t // tb
    blk = pl.BlockSpec((tb, HGRN_HEAD), lambda b, h, j: (b * nj + j, h))
    full = lambda a: pl.BlockSpec(a.shape, lambda b, h, j: (0,) * a.ndim)
    return pl.pallas_call(
        _hgrn_prompt_kernel,
        grid=(batch, heads, nj),
        in_specs=[blk, blk, blk, blk, full(tall), full(lv)],
        out_specs=[blk, pl.BlockSpec((1, 1, HGRN_HEAD, HGRN_HEAD), lambda b, h, j: (b, h, 0, 0))],
        out_shape=[jax.ShapeDtypeStruct((n, d), BF16),
                   jax.ShapeDtypeStruct((batch, heads, HGRN_HEAD, HGRN_HEAD), F32)],
        scratch_shapes=[pltpu.VMEM((HGRN_HEAD, HGRN_HEAD), F32)],
        compiler_params=_params(("arbitrary", "arbitrary", "arbitrary")),
    )(q, k, lf, v, tall, lv)


def _hgrn_decode_kernel(q_ref, k_ref, lf_ref, v_ref, s0_ref, tall_ref, lv_ref, o_ref, sout_ref, *, seq):
    q, k, lf, v = q_ref[...], k_ref[...], lf_ref[...], v_ref[...]
    v_bf = v.astype(BF16)
    o_intra, b, rev = _hgrn_intra(q, k, lf, v_bf, tall_ref, lv_ref[...], seq)
    eb = jnp.exp(b)
    qs = q * eb
    ks_t = (k * jnp.exp(rev)).T
    eb_t = eb.T
    lane = lax.broadcasted_iota(jnp.int32, ks_t.shape, 1)
    row = lax.broadcasted_iota(jnp.int32, qs.shape, 0)
    o = o_intra
    for i in range(CHUNK // seq):
        s0 = s0_ref[i, 0]
        mine = (lane >= i * seq) & (lane < (i + 1) * seq)
        o_i = _dot(qs.astype(BF16), s0.astype(BF16))
        o = o + jnp.where((row >= i * seq) & (row < (i + 1) * seq), o_i, 0.0)
        ds = _dot(jnp.where(mine, ks_t, 0.0).astype(BF16), v_bf)
        sout_ref[i, 0] = s0 * eb_t[:, (i + 1) * seq - 1:(i + 1) * seq] + ds
    o_ref[...] = o.astype(o_ref.dtype)


def _hgrn_decode(q, k, lf, v, s0_all, layer, seq, consts):
    tall, lv, _ = consts
    n, d = q.shape
    heads = d // HGRN_HEAD
    per = CHUNK // seq
    blk = pl.BlockSpec((CHUNK, HGRN_HEAD), lambda g, h: (g, h))
    sblk = pl.BlockSpec((per, 1, HGRN_HEAD, HGRN_HEAD), lambda g, h: (g, h, 0, 0))
    s0blk = pl.BlockSpec((None, per, 1, HGRN_HEAD, HGRN_HEAD), lambda g, h: (layer, g, h, 0, 0))
    return pl.pallas_call(
        functools.partial(_hgrn_decode_kernel, seq=seq),
        grid=(n // CHUNK, heads),
        in_specs=[blk, blk, blk, blk, s0blk, _const_spec(tall), _const_spec(lv)],
        out_specs=[blk, sblk],
        out_shape=[jax.ShapeDtypeStruct((n, d), BF16), jax.ShapeDtypeStruct(s0_all.shape[1:], F32)],
        compiler_params=_params(("arbitrary", "arbitrary")),
    )(q, k, lf, v, s0_all, tall, lv)


_MIX_CONSTS = {"hgrn": 2, "fox": 1}


def _mix_residual(kind, o_ref, gt_ref, x_ref, consts):
    o = o_ref[...].astype(F32)
    gt = gt_ref[...].astype(F32)
    if kind == "hgrn":
        on_ref, w_ref = consts
        parts = []
        for h in range(o.shape[1] // HGRN_HEAD):
            oh = o[:, h * HGRN_HEAD:(h + 1) * HGRN_HEAD]
            parts.append(oh * lax.rsqrt(jnp.mean(oh * oh, axis=-1, keepdims=True) + EPS))
        y = jnp.concatenate(parts, axis=-1) * on_ref[...] * (gt * _sigmoid(gt))
    else:
        (w_ref,) = consts
        y = o * _sigmoid(gt)
    return x_ref[...] + _dot(y.astype(BF16), w_ref[...])


def _ffn_tail(h, u, u1, u2, gate, cw_ref, cb_ref, wout_ref):
    a = cb_ref[...] + cw_ref[0:1, :] * u2 + cw_ref[1:2, :] * u1 + cw_ref[2:3, :] * u
    y = a * _sigmoid(a) * gate
    return h + _dot(y.astype(BF16), wout_ref[...])


def _ffn_prompt_kernel(o_ref, gt_ref, x_ref, *refs, kind):
    mix_consts = refs[:_MIX_CONSTS[kind]]
    g_ref, win_ref, cw_ref, cb_ref, wout_ref, h_ref, cs_ref, carry = refs[_MIX_CONSTS[kind]:]
    j = pl.program_id(1)
    f = cw_ref.shape[1]

    @pl.when(j == 0)
    def _():
        carry[...] = jnp.zeros_like(carry)

    h = _mix_residual(kind, o_ref, gt_ref, x_ref, mix_consts)
    tm = h.shape[0]
    ug = _dot(_rms(h, g_ref[...]).astype(BF16), win_ref[...])
    u = ug[:, :f]
    prev = carry[...]
    p0 = prev[SUBLANES - 2:SUBLANES - 1, :]
    p1 = prev[SUBLANES - 1:SUBLANES, :]
    row = lax.broadcasted_iota(jnp.int32, u.shape, 0)
    u1 = jnp.where(row == 0, p1, pltpu.roll(u, 1, axis=0))
    u2 = jnp.where(row == 0, p0, jnp.where(row == 1, p1, pltpu.roll(u, 2, axis=0)))
    h_ref[...] = _ffn_tail(h, u, u1, u2, ug[:, f:], cw_ref, cb_ref, wout_ref)
    carry[...] = u[tm - SUBLANES:, :]

    @pl.when(j == pl.num_programs(1) - 1)
    def _():
        cs_ref[0] = u[tm - (CONV_W - 1):, :]


def _ffn_prompt(kind, o, gt, x, mix_consts, ffn_consts, batch):
    n, d = x.shape
    f = _arr(ffn_consts[2]).shape[-1]
    t = n // batch
    tm = min(ROW_TILE, t)
    nj = t // tm
    row = pl.BlockSpec((tm, d), lambda b, j: (b * nj + j, 0))
    consts = list(mix_consts) + list(ffn_consts)
    return pl.pallas_call(
        functools.partial(_ffn_prompt_kernel, kind=kind),
        grid=(batch, nj),
        in_specs=[row, row, row] + [_const_spec(c) for c in consts],
        out_specs=[row, pl.BlockSpec((1, CONV_W - 1, f), lambda b, j: (b, 0, 0))],
        out_shape=[jax.ShapeDtypeStruct((n, d), F32), jax.ShapeDtypeStruct((batch, CONV_W - 1, f), F32)],
        scratch_shapes=[pltpu.VMEM((SUBLANES, f), F32)],
        compiler_params=_params(("arbitrary", "arbitrary")),
    )(o, gt, x, *[_arr(c) for c in consts])


def _ffn_decode_kernel(o_ref, gt_ref, x_ref, prev_ref, *refs, kind, seq):
    mix_consts = refs[:_MIX_CONSTS[kind]]
    g_ref, win_ref, cw_ref, cb_ref, wout_ref, h_ref, cs_ref = refs[_MIX_CONSTS[kind]:]
    f = cw_ref.shape[1]
    h = _mix_residual(kind, o_ref, gt_ref, x_ref, mix_consts)
    tm = h.shape[0]
    ug = _dot(_rms(h, g_ref[...]).astype(BF16), win_ref[...])
    u = ug[:, :f]
    u3 = u.reshape(tm // seq, seq, f)
    prev = prev_ref[...]
    p0 = prev[:, 0:1, :]
    p1 = prev[:, 1:2, :]
    row = lax.broadcasted_iota(jnp.int32, u3.shape, 1)
    u1 = jnp.where(row == 0, p1, pltpu.roll(u3, 1, axis=1))
    u2 = jnp.where(row == 0, p0, jnp.where(row == 1, p1, pltpu.roll(u3, 2, axis=1)))
    h_ref[...] = _ffn_tail(h, u, u1.reshape(tm, f), u2.reshape(tm, f), ug[:, f:], cw_ref, cb_ref, wout_ref)
    cs_ref[...] = u3[:, seq - (CONV_W - 1):, :]


def _ffn_decode(kind, o, gt, x, mix_consts, ffn_consts, prev_all, layer, seq):
    n, d = x.shape
    f = _arr(ffn_consts[2]).shape[-1]
    tm = min(ROW_TILE, n)
    row = pl.BlockSpec((tm, d), lambda i: (i, 0))
    st = pl.BlockSpec((tm // seq, CONV_W - 1, f), lambda i: (i, 0, 0))
    st_in = pl.BlockSpec((None, tm // seq, CONV_W - 1, f), lambda i: (layer, i, 0, 0))
    consts = list(mix_consts) + list(ffn_consts)
    return pl.pallas_call(
        functools.partial(_ffn_decode_kernel, kind=kind, seq=seq),
        grid=(n // tm,),
        in_specs=[row, row, row, st_in] + [_const_spec(c) for c in consts],
        out_specs=[row, st],
        out_shape=[jax.ShapeDtypeStruct((n, d), F32), jax.ShapeDtypeStruct(prev_all.shape[1:], F32)],
        compiler_params=_params(("arbitrary",)),
    )(o, gt, x, prev_all, *[_arr(c) for c in consts])


def _aug_consts(heads):
    width = heads * LANES
    pq, pk = np.zeros((3 * LANES, width), np.float32), np.zeros((3 * LANES, width), np.float32)
    oq, ok, real = (np.zeros((1, width), np.float32) for _ in range(3))
    for h in range(heads):
        e = h % 2
        free = LANES * h + FOX_HEAD * (1 - e)
        real[0, LANES * h + FOX_HEAD * e:LANES * h + FOX_HEAD * (e + 1)] = 1.0
        for x in range(3):
            pq[LANES * x + h, free + x] = 1.0
            pk[LANES * x + h, free + 3 + x] = -1.0
            oq[0, free + 3 + x] = 1.0
            ok[0, free + x] = 1.0
    return (jnp.asarray(real), (jnp.asarray(pq, BF16), jnp.asarray(oq)), (jnp.asarray(pk, BF16), jnp.asarray(ok)))


def _head_slots(x, real, bias):
    pairs = x.shape[1] // LANES
    dup = jnp.concatenate([x[:, LANES * p:LANES * (p + 1)] for p in range(pairs) for _ in range(2)], axis=1)
    return jnp.where(real > 0.0, dup, bias).astype(BF16)


def _bias_slots(c, p_ref, ones_ref):
    hi, mid, lo = _split3(c * LOG2E)
    return _dot(jnp.concatenate([hi, mid, lo], axis=1), p_ref[...]) + ones_ref[...]


def _shared_kv_kernel(x_ref, g_ref, wk_ref, wv_ref, wf_ref, bf_ref, kn_ref, bd_ref, tri_ref, *rest, aug):
    if aug:
        real_ref, pk_ref, ok_ref, k_ref, v_ref, lf_ref, c_ref, ka_ref, va_ref, carry = rest
    else:
        k_ref, v_ref, lf_ref, c_ref, carry = rest
    j = pl.program_id(1)
    heads = lf_ref.shape[1]

    @pl.when(j == 0)
    def _():
        carry[...] = jnp.zeros_like(carry)

    hn = _rms(x_ref[...], g_ref[...]).astype(BF16)
    k = _group_rms(_dot(hn, wk_ref[...]), bd_ref, FOX_HEAD) * kn_ref[...]
    v = _dot(hn, wv_ref[...])
    lf = _log_sigmoid(_dot(hn, wf_ref[...]) + bf_ref[...])
    c = _dot3_right(tri_ref[...], lf) + carry[...]
    carry[...] = c[c.shape[0] - 1:, :]
    lf_ref[...] = lf[:, :heads]
    c_ref[...] = c
    if aug:
        tm = k.shape[0]
        k_ref[0] = k.T.reshape(heads, FOX_HEAD, tm)
        v_ref[0] = v.T.reshape(heads, FOX_HEAD, tm)
        real = real_ref[...]
        ka_ref[...] = _head_slots(k, real, _bias_slots(c, pk_ref, ok_ref))
        va_ref[...] = _head_slots(v, real, 1.0)
    else:
        k_ref[...] = k
        v_ref[...] = v


def _shared_kv(x, g, wk, wv, wf, bf, kn, bd, tri, groups, heads, aug=None):
    n, d = x.shape
    tm = tri.shape[0]
    nj = n // groups // tm
    row = lambda w: pl.BlockSpec((tm, w), lambda b, j: (b * nj + j, 0))
    consts = [g, wk, wv, wf, bf, kn, bd, tri]
    tail_specs = [row(heads), row(LANES)]
    tail_shape = [jax.ShapeDtypeStruct((n, heads), F32), jax.ShapeDtypeStruct((n, LANES), F32)]
    if aug is None:
        extra = []
        out_specs = [row(d), row(d)] + tail_specs
        out_shape = [jax.ShapeDtypeStruct((n, d), F32)] * 2 + tail_shape
    else:
        real, _, (pk, ok) = aug
        extra = [real, pk, ok]
        kv_t = pl.BlockSpec((1, heads, FOX_HEAD, tm), lambda b, j: (b, 0, 0, j))
        out_specs = [kv_t, kv_t] + tail_specs + [row(heads * LANES)] * 2
        out_shape = ([jax.ShapeDtypeStruct((groups, heads, FOX_HEAD, nj * tm), F32)] * 2 + tail_shape
                     + [jax.ShapeDtypeStruct((n, heads * LANES), BF16)] * 2)
    return pl.pallas_call(
        functools.partial(_shared_kv_kernel, aug=aug is not None),
        grid=(groups, nj),
        in_specs=[row(d)] + [_const_spec(a) for a in consts + extra],
        out_specs=out_specs,
        out_shape=out_shape,
        scratch_shapes=[pltpu.VMEM((1, LANES), F32)],
        compiler_params=_params(("arbitrary", "arbitrary")),
    )(x, *consts, *extra)


def _fox_qg_kernel(x_ref, g_ref, w_ref, qn_ref, bd_ref, *rest, aug):
    d = x_ref.shape[1]
    qg = _dot(_rms(x_ref[...], g_ref[...]).astype(BF16), w_ref[...])
    q = _group_rms(qg[:, :d], bd_ref, FOX_HEAD) * (qn_ref[...] * FOX_HEAD ** -0.5)
    if aug:
        c_ref, real_ref, pq_ref, oq_ref, q_ref, gt_ref = rest
        q_ref[...] = _head_slots(q * LOG2E, real_ref[...], _bias_slots(c_ref[...], pq_ref, oq_ref))
    else:
        q_ref, gt_ref = rest
        q_ref[...] = q
    gt_ref[...] = qg[:, d:].astype(BF16)


def _fox_qg(x, g, w, qn, bd, c=None, aug=None):
    n, d = x.shape
    tm = min(ROW_TILE, n)
    row = lambda w_: pl.BlockSpec((tm, w_), lambda i: (i, 0))
    full = lambda a: pl.BlockSpec(a.shape, lambda i: (0,) * a.ndim)
    consts = [g, w, qn, bd]
    if aug is None:
        extra, extra_specs = [], []
        q_spec, q_shape = row(d), jax.ShapeDtypeStruct((n, d), F32)
    else:
        real, (pq, oq), _ = aug
        extra, extra_specs = [c, real, pq, oq], [row(LANES), full(real), full(pq), full(oq)]
        width = real.shape[1]
        q_spec, q_shape = row(width), jax.ShapeDtypeStruct((n, width), BF16)
    return pl.pallas_call(
        functools.partial(_fox_qg_kernel, aug=aug is not None),
        grid=(n // tm,),
        in_specs=[row(d)] + [_const_spec(a) for a in consts] + extra_specs,
        out_specs=[q_spec, row(d)],
        out_shape=[q_shape, jax.ShapeDtypeStruct((n, d), BF16)],
        compiler_params=_params(("arbitrary",)),
    )(x, *[_arr(a) for a in consts], *extra)


def _attn_prompt_kernel(q_ref, k_ref, v_ref, o_ref):
    i = pl.program_id(2)
    tq = q_ref.shape[0]

    def slot(ref, rows, e):
        return ref[rows, e * LANES:(e + 1) * LANES]

    def block(j, state, keep):
        rows = pl.ds(pl.multiple_of(j * tq, tq), tq)
        out = []
        for e in range(2):
            m, acc = state[2 * e:2 * e + 2]
            s = _dot_nt(slot(q_ref, slice(None), e), slot(k_ref, rows, e))
            if keep is not None:
                s = jnp.where(keep, s, -jnp.inf)
            m_new = jnp.maximum(m, jnp.max(s, axis=-1, keepdims=True))
            p = jnp.exp2(s - m_new)
            out += [m_new, jnp.exp2(m - m_new) * acc + _dot(p.astype(BF16), slot(v_ref, rows, e))]
        return tuple(out)

    init = (jnp.full((tq, 1), -jnp.inf, F32), jnp.zeros((tq, LANES), F32)) * 2
    state = lax.fori_loop(0, i, lambda j, st: block(j, st, None), init)
    causal = lax.broadcasted_iota(jnp.int32, (tq, tq), 1) <= lax.broadcasted_iota(jnp.int32, (tq, tq), 0)
    state = block(i, state, causal)
    outs = [acc / pltpu.roll(acc, FOX_HEAD, axis=1) for acc in (state[1], state[3])]
    lane = lax.broadcasted_iota(jnp.int32, (tq, LANES), 1)
    o_ref[...] = jnp.where(lane < FOX_HEAD, outs[0], outs[1]).astype(o_ref.dtype)


def _attn_prompt(qa, ka, va, batch):
    n, width = qa.shape
    t = n // batch
    tq = min(ATTN_TILE, t)
    nq = t // tq
    pairs = width // (2 * LANES)
    return pl.pallas_call(
        _attn_prompt_kernel,
        grid=(batch, pairs, nq),
        in_specs=[pl.BlockSpec((tq, 2 * LANES), lambda b, p, i: (b * nq + i, p)),
                  pl.BlockSpec((t, 2 * LANES), lambda b, p, i: (b, p)),
                  pl.BlockSpec((t, 2 * LANES), lambda b, p, i: (b, p))],
        out_specs=pl.BlockSpec((tq, LANES), lambda b, p, i: (b * nq + i, p)),
        out_shape=jax.ShapeDtypeStruct((n, pairs * LANES), BF16),
        compiler_params=_params(("arbitrary", "arbitrary", "arbitrary")),
    )(qa, ka, va)


def _page_prefix(n_pages, heads):
    r = np.arange(n_pages * heads)
    return jnp.asarray(((r[:, None] % heads) == (r[None, :] % heads)) & ((r[None, :] // heads) < (r[:, None] // heads)),
                       BF16)


def _attn_decode_kernel(pt_ref, q_ref, *refs, n_pages):
    del pt_ref
    kp, vp, lp = refs[:n_pages], refs[n_pages:2 * n_pages], refs[2 * n_pages:3 * n_pages]
    kn_ref, vn_ref, cn_ref, cq_ref, hm_ref, u_ref, px_ref, o_ref = refs[3 * n_pages:]
    seq, d = q_ref.shape[1:]
    hm = hm_ref[...]
    heads = hm.shape[0]
    rows = seq * heads
    q = (q_ref[0][:, None, :] * hm[None, :, :]).reshape(rows, d).astype(BF16)

    c_page = _dot3_left(jnp.concatenate([lp[g][0] for g in range(n_pages)], axis=0), u_ref[...])
    page_total = jnp.broadcast_to(c_page[:, PAGE - 1:PAGE], c_page.shape)
    before = _dot3_right(px_ref[...], page_total)
    c_past = c_page + before
    total = (before + page_total)[(n_pages - 1) * heads:, :]
    cq = cq_ref[0] + jnp.concatenate([total] * seq, axis=0)

    def update(state, s, ck, keep, pv):
        m, l, acc = state
        s = s + (jnp.concatenate([cq] * (s.shape[1] // PAGE), axis=1) - jnp.concatenate([ck] * seq, axis=0))
        if keep is not None:
            s = jnp.where(keep, s, -jnp.inf)
        m_new = jnp.maximum(m, jnp.max(s, axis=-1, keepdims=True))
        pr = jnp.exp(s - m_new)
        alpha = jnp.exp(m - m_new)
        return m_new, alpha * l + jnp.sum(pr, axis=-1, keepdims=True), alpha * acc + pv(pr.astype(BF16))

    state = (jnp.full((rows, 1), -jnp.inf, F32), jnp.zeros((rows, 1), F32), jnp.zeros((rows, d), F32))
    for g0 in range(0, n_pages, DECODE_PAGES):
        group = range(g0, g0 + DECODE_PAGES)
        k_cat = jnp.concatenate([kp[g][0].astype(BF16) for g in group], axis=1)
        v_cat = jnp.concatenate([vp[g][0].astype(BF16) for g in group], axis=1)
        ck = jnp.concatenate([c_past[g * heads:(g + 1) * heads, :] for g in group], axis=1)
        state = update(state, _dot(q, k_cat), ck, None, lambda pr, v_cat=v_cat: _dot_nt(pr, v_cat))

    pad = jnp.zeros((PAGE - seq, d), F32)
    k_new = jnp.concatenate([kn_ref[0], pad], axis=0).astype(BF16)
    v_new = jnp.concatenate([vn_ref[0], pad], axis=0).astype(BF16)
    t = lax.broadcasted_iota(jnp.int32, (rows, PAGE), 0) // heads
    key = lax.broadcasted_iota(jnp.int32, (rows, PAGE), 1)
    _, l, acc = update(state, _dot_nt(q, k_new), cn_ref[0] + total, key <= t, lambda pr: _dot(pr, v_new))
    o_ref[0] = jnp.sum((acc / l).reshape(seq, heads, d) * hm[None, :, :], axis=1)


def _attn_decode(q, cache_kt, cache_vt, cache_lt, k_new, v_new, c_new_t, cq_col, head_mask, upper, prefix,
                 page_table):
    bd, seq, d = q.shape
    n_pages = page_table.shape[1]
    heads = head_mask.shape[0]
    page = lambda g, w: pl.BlockSpec((1, w, PAGE), lambda s, pt: (pt[s * n_pages + g], 0, 0))
    tok = pl.BlockSpec((1, seq, d), lambda s, pt: (s, 0, 0))
    grid_spec = pltpu.PrefetchScalarGridSpec(
        num_scalar_prefetch=1,
        grid=(bd,),
        in_specs=[tok] + [page(g, d) for g in range(n_pages)] + [page(g, d) for g in range(n_pages)]
        + [page(g, heads) for g in range(n_pages)] + [
            tok, tok,
            pl.BlockSpec((1, heads, PAGE), lambda s, pt: (s, 0, 0)),
            pl.BlockSpec((1, seq * heads, 1), lambda s, pt: (s, 0, 0)),
            _const_spec(head_mask), _const_spec(upper), _const_spec(prefix)],
        out_specs=tok,
    )
    return pl.pallas_call(
        functools.partial(_attn_decode_kernel, n_pages=n_pages),
        grid_spec=grid_spec,
        out_shape=jax.ShapeDtypeStruct((bd, seq, d), F32),
        compiler_params=_params(("arbitrary",)),
    )(page_table.reshape(-1), q, *([cache_kt] * n_pages), *([cache_vt] * n_pages), *([cache_lt] * n_pages),
      k_new, v_new, c_new_t, cq_col, head_mask, upper, prefix)


def _trunk(x, seq, hgrn_s0, conv_s0, past, w):
    n, d = x.shape
    n_seq = n // seq
    depth = w["ffn_w_in"].shape[0]
    n_a = w["hgrn_w_in"].shape[0]
    heads = w["fox_b_f"].shape[0]
    prompt = past is None
    h = x
    new_hgrn, new_conv = [], []
    for layer in range(depth):
        if layer < n_a:
            q, k, lf, i, gt = _hgrn_in(h, (w["norm_mix"], layer), (w["hgrn_w_in"], layer), w["hgrn_lb_param"], layer)
            if prompt:
                o, s = _hgrn_prompt(q, k, lf, i, n_seq, w["hgrn_prompt_consts"])
            else:
                o, s = _hgrn_decode(q, k, lf, i, hgrn_s0, layer, seq, w["hgrn_decode_consts"])
            new_hgrn.append(s)
            mix = ("hgrn", o, gt, h, [(w["hgrn_onorm"], layer), (w["hgrn_w_o"], layer)])
        else:
            j = layer - n_a
            qg = (h, (w["norm_mix"], layer), (w["fox_w_qg"], j), (w["q_norm"], j), w["bd_fox"])
            if prompt:
                qa, gt = _fox_qg(*qg, c, w["aug"])
                o = _attn_prompt(qa, ka, va, n_seq)
            else:
                q, gt = _fox_qg(*qg)
                o = _attn_decode(q.reshape(n_seq, seq, d), *past[:3], k_sh.reshape(n_seq, seq, d),
                                 v_sh.reshape(n_seq, seq, d), c_new_t, cq_col, w["head_mask"], w["upper"],
                                 w["page_prefix"], past[3]).reshape(n, d)
            mix = ("fox", o, gt, h, [(w["fox_w_o"], j)])
        ffn = [(w[name], layer) for name in ("norm_ffn", "ffn_w_in", "ffn_conv_w", "ffn_conv_b", "ffn_w_out")]
        if prompt:
            h, cs = _ffn_prompt(*mix, ffn, n_seq)
        else:
            h, cs = _ffn_decode(*mix, ffn, conv_s0, layer, seq)
        new_conv.append(cs)
        if layer == n_a - 1:
            tri = w["tri_prompt"] if prompt else w["tri_decode"]
            kv = _shared_kv(h, w["kv_norm"], w["w_k"], w["w_v"], w["w_f"], w["b_f"], w["k_norm"], w["bd_fox"], tri,
                            n_seq if prompt else n // tri.shape[0], heads, w["aug"] if prompt else None)
            k_sh, v_sh, logf, c = kv[:4]
            if prompt:
                ka, va = kv[4:]
            else:
                c3 = c[:, :heads].reshape(n_seq, seq, heads)
                cq_col = c3.reshape(n_seq, seq * heads, 1)
                c_new_t = jnp.pad(c3.transpose(0, 2, 1), ((0, 0), (0, 0), (0, PAGE - seq)))
    return h, k_sh, v_sh, logf, jnp.stack(new_hgrn), jnp.stack(new_conv)


def kernel(x_prompt, x_sample, cache_k, cache_v, cache_logf, state_hgrn, state_conv, page_table, norm_mix, norm_ffn, hgrn_w_in, hgrn_lb_param, hgrn_onorm, hgrn_w_o, kv_norm, w_kvf, fox_b_f, k_norm, fox_w_qg, q_norm, fox_w_o, ffn_w_in, ffn_conv_w, ffn_conv_b, ffn_w_out):
    b, t, d = x_prompt.shape
    bd, ts, _ = x_sample.shape
    heads = fox_b_f.shape[0]
    n_phys, page = cache_k.shape[:2]
    assert page == PAGE and d % LANES == 0 and d // heads == FOX_HEAD
    assert t % ATTN_TILE == 0 and page_table.shape[1] % DECODE_PAGES == 0
    assert t % ROW_TILE == 0 and (bd * ts) % ROW_TILE == 0 and CHUNK % ts == 0 and ROW_TILE % ts == 0
    assert ts >= CONV_W - 1

    row = lambda a: a.reshape(a.shape[:-1] + (1, a.shape[-1]))
    w = {
        "norm_mix": row(norm_mix), "norm_ffn": row(norm_ffn),
        "hgrn_w_in": hgrn_w_in.astype(BF16), "hgrn_lb_param": hgrn_lb_param,
        "hgrn_onorm": row(jnp.tile(hgrn_onorm, (1, d // HGRN_HEAD))), "hgrn_w_o": hgrn_w_o.astype(BF16),
        "kv_norm": row(kv_norm),
        "w_k": w_kvf[:, :d].astype(BF16), "w_v": w_kvf[:, d:2 * d].astype(BF16),
        "w_f": jnp.pad(w_kvf[:, 2 * d:], ((0, 0), (0, LANES - heads))).astype(BF16),
        "b_f": jnp.pad(fox_b_f, (0, LANES - heads)).reshape(1, LANES), "fox_b_f": fox_b_f,
        "k_norm": jnp.tile(k_norm, heads).reshape(1, d),
        "fox_w_qg": fox_w_qg.astype(BF16), "q_norm": row(jnp.tile(q_norm, (1, heads))),
        "fox_w_o": fox_w_o.astype(BF16),
        "ffn_w_in": ffn_w_in.astype(BF16), "ffn_conv_w": ffn_conv_w, "ffn_conv_b": row(ffn_conv_b),
        "ffn_w_out": ffn_w_out.astype(BF16),
        "hgrn_prompt_consts": _level_mats(CHUNK, CHUNK), "hgrn_decode_consts": _level_mats(CHUNK, ts),
        "bd_fox": _block_diag_ones(d, FOX_HEAD),
        "tri_prompt": _seg_lower_tri(ROW_TILE, ROW_TILE), "tri_decode": _seg_lower_tri(ROW_TILE, ts),
        "upper": _upper_tri(PAGE), "head_mask": _head_mask(heads, FOX_HEAD),
        "aug": _aug_consts(heads), "page_prefix": _page_prefix(page_table.shape[1], heads),
    }

    y_p, k_p, v_p, lf_p, hgrn_p, conv_p = _trunk(x_prompt.reshape(b * t, d), t, None, None, None, w)

    past = (cache_k.transpose(0, 2, 3, 1).reshape(n_phys, d, page), cache_v.transpose(0, 2, 3, 1).reshape(n_phys, d, page),
            cache_logf.astype(F32).transpose(0, 2, 1), page_table)
    y_s, k_s, v_s, lf_s, hgrn_s, conv_s = _trunk(x_sample.reshape(bd * ts, d), ts, state_hgrn, state_conv, past, w)

    hd = (heads, d // heads)
    return (y_p.reshape(b, t, d), y_s.reshape(bd, ts, d),
            k_p.transpose(0, 3, 1, 2), v_p.transpose(0, 3, 1, 2), lf_p.reshape(b, t, heads),
            k_s.reshape((bd, ts) + hd), v_s.reshape((bd, ts) + hd), lf_s.reshape(bd, ts, heads),
            hgrn_p, hgrn_s, conv_p, conv_s)
```

```python
import functools

import numpy as np
import jax
import jax.numpy as jnp
from jax import lax
from jax.experimental import pallas as pl
from jax.experimental.pallas import tpu as pltpu

F32 = jnp.float32
BF16 = jnp.bfloat16
EPS = 1e-6

LANES = 128
SUBLANES = 8
VMEM_LIMIT_BYTES = 56 * 1024 * 1024

HGRN_HEAD = 128
FOX_HEAD = 64
CONV_W = 3
CHUNK = 128
ROW_TILE = 512
ATTN_TILE = 512
PAGE = 128
DECODE_PAGES = 4
LOG2E = 1.4426950408889634


def _params(semantics):
    return pltpu.CompilerParams(dimension_semantics=semantics, vmem_limit_bytes=VMEM_LIMIT_BYTES)


def _const_spec(a):
    if isinstance(a, tuple):
        arr, layer = a
        return pl.BlockSpec((None,) + arr.shape[1:], lambda *_: (layer,) + (0,) * (arr.ndim - 1))
    return pl.BlockSpec(a.shape, lambda *_: (0,) * a.ndim)


def _arr(a):
    return a[0] if isinstance(a, tuple) else a


def _dot(a, b):
    return jnp.dot(a, b, preferred_element_type=F32)


def _dot_nt(a, b):
    return lax.dot_general(a, b, (((1,), (1,)), ((), ())), preferred_element_type=F32)


def _split3(x):
    hi = x.astype(BF16)
    r = x - hi.astype(F32)
    mid = r.astype(BF16)
    lo = (r - mid.astype(F32)).astype(BF16)
    return hi, mid, lo


def _dot3_right(t, x):
    hi, mid, lo = _split3(x)
    return _dot(t, hi) + _dot(t, mid) + _dot(t, lo)


def _dot3_left(x, t):
    hi, mid, lo = _split3(x)
    return _dot(hi, t) + _dot(mid, t) + _dot(lo, t)


def _rms(x, g):
    r = lax.rsqrt(jnp.mean(x * x, axis=-1, keepdims=True) + EPS)
    return x * r * g


def _sigmoid(x):
    return 1.0 / (1.0 + jnp.exp(-x))


def _log_sigmoid(x):
    return jnp.minimum(x, 0.0) - jnp.log(1.0 + jnp.exp(-jnp.abs(x)))


def _group_rms(x, bd_ref, width):
    ss = _dot((x * x).astype(BF16), bd_ref[...])
    return x * lax.rsqrt(ss * (1.0 / width) + EPS)


def _level_mats(rows, seg):
    t = np.arange(rows)[:, None]
    j = np.arange(rows)[None, :]
    mats = [((t // seg) == (j // seg)) & (j <= t)]
    lv = np.full((rows, rows), -1, np.int32)
    lv[np.arange(rows), np.arange(rows)] = 0
    h, level = seg // 2, 1
    while h >= 1:
        blk = t // (2 * h)
        m = blk * 2 * h + h
        in_blk = (j // (2 * h)) == blk
        upper = t >= m
        if h < SUBLANES:
            mats.append(in_blk & np.where(upper, (j >= m) & (j <= t), (j >= t + 1) & (j <= m - 1)))
        lv[in_blk & upper & ((j % (2 * h)) < h)] = level
        h //= 2
        level += 1
    tall = np.concatenate([m.astype(np.float32) for m in mats], axis=0)
    return jnp.asarray(tall, BF16), jnp.asarray(lv), seg


def _block_diag_ones(n, width):
    i = np.arange(n)
    return jnp.asarray((i[:, None] // width) == (i[None, :] // width), BF16)


def _seg_lower_tri(n, seg):
    i = np.arange(n)
    return jnp.asarray(((i[:, None] // seg) == (i[None, :] // seg)) & (i[None, :] <= i[:, None]), BF16)


def _upper_tri(n):
    i = np.arange(n)
    return jnp.asarray(i[:, None] <= i[None, :], BF16)


def _head_mask(heads, width):
    lane = np.arange(heads * width)
    return jnp.asarray((lane[None, :] // width) == np.arange(heads)[:, None], F32)


def _hgrn_in_kernel(x_ref, g_ref, w_ref, lbp_ref, q_ref, k_ref, lf_ref, i_ref, gt_ref, *, layer):
    d = q_ref.shape[1]
    hn = _rms(x_ref[...], g_ref[...]).astype(BF16)
    proj = _dot(hn, w_ref[...])
    q = proj[:, :d]
    fr = proj[:, d:2 * d]
    q_ref[...] = (q * _sigmoid(q)).astype(BF16)
    i_ref[...] = proj[:, 2 * d:3 * d].astype(BF16)
    gt_ref[...] = proj[:, 3 * d:].astype(BF16)
    if layer == 0:
        lf_ref[...] = _log_sigmoid(fr)
        k_ref[...] = _sigmoid(-fr).astype(BF16)
    else:
        lbp = lbp_ref[...]
        e = jnp.exp(lbp - jnp.max(lbp, axis=0, keepdims=True))
        p = e / jnp.sum(e, axis=0, keepdims=True)
        lb = jnp.sum(p[1:layer + 1], axis=0, keepdims=True)
        lf_ref[...] = jnp.log(lb + (1.0 - lb) * _sigmoid(fr))
        k_ref[...] = ((1.0 - lb) * _sigmoid(-fr)).astype(BF16)


def _hgrn_in(x, g, w, lbp, layer):
    n, d = x.shape
    tm = min(ROW_TILE, n)
    row = pl.BlockSpec((tm, d), lambda i: (i, 0))
    out = lambda dtype: jax.ShapeDtypeStruct((n, d), dtype)
    return pl.pallas_call(
        functools.partial(_hgrn_in_kernel, layer=layer),
        grid=(n // tm,),
        in_specs=[row, _const_spec(g), _const_spec(w), _const_spec(lbp)],
        out_specs=[row] * 5,
        out_shape=[out(BF16), out(BF16), out(F32), out(BF16), out(BF16)],
        compiler_params=_params(("arbitrary",)),
    )(x, _arr(g), _arr(w), lbp)


def _hgrn_intra(q, k, lf, v_bf, tall_ref, lv, seg):
    c, width = q.shape
    hi, mid, lo = _split3(lf)
    d3 = _dot(tall_ref[...], jnp.concatenate([hi, mid, lo], axis=1))
    d = d3[:, :c] + d3[:, c:2 * c] + d3[:, 2 * c:]
    b = d[:c]
    a = jnp.where(lv == 0, _dot_nt(q.astype(BF16), k.astype(BF16)), 0.0)
    h, level, blk = seg // 2, 1, 1
    while h >= 1:
        if h >= SUBLANES:
            bh = b.reshape(c // (2 * h), 2 * h, width)
            e = jnp.exp(-jnp.abs(bh - bh[:, h - 1:h, :])).reshape(c, width)
        else:
            e = jnp.exp(d[blk * c:(blk + 1) * c])
            blk += 1
        a = jnp.where(lv == level, _dot_nt((q * e).astype(BF16), (k * e).astype(BF16)), a)
        h //= 2
        level += 1
    bs = b.reshape(c // seg, seg, width)
    return _dot(a.astype(BF16), v_bf), b, (bs[:, seg - 1:seg, :] - bs).reshape(c, width)


def _hgrn_prompt_kernel(q_ref, k_ref, lf_ref, v_ref, tall_ref, lv_ref, o_ref, sout_ref, s_scr):
    j = pl.program_id(2)

    @pl.when(j == 0)
    def _():
        s_scr[...] = jnp.zeros_like(s_scr)

    lv = lv_ref[...]
    s = s_scr[...]
    for c in range(q_ref.shape[0] // CHUNK):
        rows = pl.ds(c * CHUNK, CHUNK)
        q, k, lf, v = q_ref[rows, :], k_ref[rows, :], lf_ref[rows, :], v_ref[rows, :]
        v_bf = v.astype(BF16)
        o_intra, b, rev = _hgrn_intra(q, k, lf, v_bf, tall_ref, lv, CHUNK)
        eb = jnp.exp(b)
        o_ref[rows, :] = (o_intra + _dot((q * eb).astype(BF16), s.astype(BF16))).astype(o_ref.dtype)
        ks_t = (k * jnp.exp(rev)).T.astype(BF16)
        decay = eb.T[:, CHUNK - 1:CHUNK]
        s = s * decay + _dot(ks_t, v_bf)
    s_scr[...] = s

    @pl.when(j == pl.num_programs(2) - 1)
    def _():
        sout_ref[0, 0] = s


def _hgrn_prompt(q, k, lf, v, batch, consts):
    tall, lv, _ = consts
    n, d = q.shape
    heads = d // HGRN_HEAD
    t = n // batch
    tb = min(8 * CHUNK, t)
    nj = t // tb
    blk = pl.BlockSpec((tb, HGRN_HEAD), lambda b, h, j: (b * nj + j, h))
    full = lambda a: pl.BlockSpec(a.shape, lambda b, h, j: (0,) * a.ndim)
    return pl.pallas_call(
        _hgrn_prompt_kernel,
        grid=(batch, heads, nj),
        in_specs=[blk, blk, blk, blk, full(tall), full(lv)],
        out_specs=[blk, pl.BlockSpec((1, 1, HGRN_HEAD, HGRN_HEAD), lambda b, h, j: (b, h, 0, 0))],
        out_shape=[jax.ShapeDtypeStruct((n, d), BF16),
                   jax.ShapeDtypeStruct((batch, heads, HGRN_HEAD, HGRN_HEAD), F32)],
        scratch_shapes=[pltpu.VMEM((HGRN_HEAD, HGRN_HEAD), F32)],
        compiler_params=_params(("arbitrary", "arbitrary", "arbitrary")),
    )(q, k, lf, v, tall, lv)


def _hgrn_decode_kernel(q_ref, k_ref, lf_ref, v_ref, s0_ref, tall_ref, lv_ref, o_ref, sout_ref, *, seq):
    q, k, lf, v = q_ref[...], k_ref[...], lf_ref[...], v_ref[...]
    v_bf = v.astype(BF16)
    o_intra, b, rev = _hgrn_intra(q, k, lf, v_bf, tall_ref, lv_ref[...], seq)
    eb = jnp.exp(b)
    qs = q * eb
    ks_t = (k * jnp.exp(rev)).T
    eb_t = eb.T
    lane = lax.broadcasted_iota(jnp.int32, ks_t.shape, 1)
    row = lax.broadcasted_iota(jnp.int32, qs.shape, 0)
    o = o_intra
    for i in range(CHUNK // seq):
        s0 = s0_ref[i, 0]
        mine = (lane >= i * seq) & (lane < (i + 1) * seq)
        o_i = _dot(qs.astype(BF16), s0.astype(BF16))
        o = o + jnp.where((row >= i * seq) & (row < (i + 1) * seq), o_i, 0.0)
        ds = _dot(jnp.where(mine, ks_t, 0.0).astype(BF16), v_bf)
        sout_ref[i, 0] = s0 * eb_t[:, (i + 1) * seq - 1:(i + 1) * seq] + ds
    o_ref[...] = o.astype(o_ref.dtype)


def _hgrn_decode(q, k, lf, v, s0_all, layer, seq, consts):
    tall, lv, _ = consts
    n, d = q.shape
    heads = d // HGRN_HEAD
    per = CHUNK // seq
    blk = pl.BlockSpec((CHUNK, HGRN_HEAD), lambda g, h: (g, h))
    sblk = pl.BlockSpec((per, 1, HGRN_HEAD, HGRN_HEAD), lambda g, h: (g, h, 0, 0))
    s0blk = pl.BlockSpec((None, per, 1, HGRN_HEAD, HGRN_HEAD), lambda g, h: (layer, g, h, 0, 0))
    return pl.pallas_call(
        functools.partial(_hgrn_decode_kernel, seq=seq),
        grid=(n // CHUNK, heads),
        in_specs=[blk, blk, blk, blk, s0blk, _const_spec(tall), _const_spec(lv)],
        out_specs=[blk, sblk],
        out_shape=[jax.ShapeDtypeStruct((n, d), BF16), jax.ShapeDtypeStruct(s0_all.shape[1:], F32)],
        compiler_params=_params(("arbitrary", "arbitrary")),
    )(q, k, lf, v, s0_all, tall, lv)


_MIX_CONSTS = {"hgrn": 2, "fox": 1}


def _mix_residual(kind, o_ref, gt_ref, x_ref, consts):
    o = o_ref[...].astype(F32)
    gt = gt_ref[...].astype(F32)
    if kind == "hgrn":
        on_ref, w_ref = consts
        parts = []
        for h in range(o.shape[1] // HGRN_HEAD):
            oh = o[:, h * HGRN_HEAD:(h + 1) * HGRN_HEAD]
            parts.append(oh * lax.rsqrt(jnp.mean(oh * oh, axis=-1, keepdims=True) + EPS))
        y = jnp.concatenate(parts, axis=-1) * on_ref[...] * (gt * _sigmoid(gt))
    else:
        (w_ref,) = consts
        y = o * _sigmoid(gt)
    return x_ref[...] + _dot(y.astype(BF16), w_ref[...])


def _ffn_tail(h, u, u1, u2, gate, cw_ref, cb_ref, wout_ref):
    a = cb_ref[...] + cw_ref[0:1, :] * u2 + cw_ref[1:2, :] * u1 + cw_ref[2:3, :] * u
    y = a * _sigmoid(a) * gate
    return h + _dot(y.astype(BF16), wout_ref[...])


def _ffn_prompt_kernel(o_ref, gt_ref, x_ref, *refs, kind):
    mix_consts = refs[:_MIX_CONSTS[kind]]
    g_ref, win_ref, cw_ref, cb_ref, wout_ref, h_ref, cs_ref, carry = refs[_MIX_CONSTS[kind]:]
    j = pl.program_id(1)
    f = cw_ref.shape[1]

    @pl.when(j == 0)
    def _():
        carry[...] = jnp.zeros_like(carry)

    h = _mix_residual(kind, o_ref, gt_ref, x_ref, mix_consts)
    tm = h.shape[0]
    ug = _dot(_rms(h, g_ref[...]).astype(BF16), win_ref[...])
    u = ug[:, :f]
    prev = carry[...]
    p0 = prev[SUBLANES - 2:SUBLANES - 1, :]
    p1 = prev[SUBLANES - 1:SUBLANES, :]
    row = lax.broadcasted_iota(jnp.int32, u.shape, 0)
    u1 = jnp.where(row == 0, p1, pltpu.roll(u, 1, axis=0))
    u2 = jnp.where(row == 0, p0, jnp.where(row == 1, p1, pltpu.roll(u, 2, axis=0)))
    h_ref[...] = _ffn_tail(h, u, u1, u2, ug[:, f:], cw_ref, cb_ref, wout_ref)
    carry[...] = u[tm - SUBLANES:, :]

    @pl.when(j == pl.num_programs(1) - 1)
    def _():
        cs_ref[0] = u[tm - (CONV_W - 1):, :]


def _ffn_prompt(kind, o, gt, x, mix_consts, ffn_consts, batch):
    n, d = x.shape
    f = _arr(ffn_consts[2]).shape[-1]
    t = n // batch
    tm = min(ROW_TILE, t)
    nj = t // tm
    row = pl.BlockSpec((tm, d), lambda b, j: (b * nj + j, 0))
    consts = list(mix_consts) + list(ffn_consts)
    return pl.pallas_call(
        functools.partial(_ffn_prompt_kernel, kind=kind),
        grid=(batch, nj),
        in_specs=[row, row, row] + [_const_spec(c) for c in consts],
        out_specs=[row, pl.BlockSpec((1, CONV_W - 1, f), lambda b, j: (b, 0, 0))],
        out_shape=[jax.ShapeDtypeStruct((n, d), F32), jax.ShapeDtypeStruct((batch, CONV_W - 1, f), F32)],
        scratch_shapes=[pltpu.VMEM((SUBLANES, f), F32)],
        compiler_params=_params(("arbitrary", "arbitrary")),
    )(o, gt, x, *[_arr(c) for c in consts])


def _ffn_decode_kernel(o_ref, gt_ref, x_ref, prev_ref, *refs, kind, seq):
    mix_consts = refs[:_MIX_CONSTS[kind]]
    g_ref, win_ref, cw_ref, cb_ref, wout_ref, h_ref, cs_ref = refs[_MIX_CONSTS[kind]:]
    f = cw_ref.shape[1]
    h = _mix_residual(kind, o_ref, gt_ref, x_ref, mix_consts)
    tm = h.shape[0]
    ug = _dot(_rms(h, g_ref[...]).astype(BF16), win_ref[...])
    u = ug[:, :f]
    u3 = u.reshape(tm // seq, seq, f)
    prev = prev_ref[...]
    p0 = prev[:, 0:1, :]
    p1 = prev[:, 1:2, :]
    row = lax.broadcasted_iota(jnp.int32, u3.shape, 1)
    u1 = jnp.where(row == 0, p1, pltpu.roll(u3, 1, axis=1))
    u2 = jnp.where(row == 0, p0, jnp.where(row == 1, p1, pltpu.roll(u3, 2, axis=1)))
    h_ref[...] = _ffn_tail(h, u, u1.reshape(tm, f), u2.reshape(tm, f), ug[:, f:], cw_ref, cb_ref, wout_ref)
    cs_ref[...] = u3[:, seq - (CONV_W - 1):, :]


def _ffn_decode(kind, o, gt, x, mix_consts, ffn_consts, prev_all, layer, seq):
    n, d = x.shape
    f = _arr(ffn_consts[2]).shape[-1]
    tm = min(ROW_TILE, n)
    row = pl.BlockSpec((tm, d), lambda i: (i, 0))
    st = pl.BlockSpec((tm // seq, CONV_W - 1, f), lambda i: (i, 0, 0))
    st_in = pl.BlockSpec((None, tm // seq, CONV_W - 1, f), lambda i: (layer, i, 0, 0))
    consts = list(mix_consts) + list(ffn_consts)
    return pl.pallas_call(
        functools.partial(_ffn_decode_kernel, kind=kind, seq=seq),
        grid=(n // tm,),
        in_specs=[row, row, row, st_in] + [_const_spec(c) for c in consts],
        out_specs=[row, st],
        out_shape=[jax.ShapeDtypeStruct((n, d), F32), jax.ShapeDtypeStruct(prev_all.shape[1:], F32)],
        compiler_params=_params(("arbitrary",)),
    )(o, gt, x, prev_all, *[_arr(c) for c in consts])


def _aug_consts(heads):
    width = heads * LANES
    pq, pk = np.zeros((3 * LANES, width), np.float32), np.zeros((3 * LANES, width), np.float32)
    oq, ok, real = (np.zeros((1, width), np.float32) for _ in range(3))
    for h in range(heads):
        e = h % 2
        free = LANES * h + FOX_HEAD * (1 - e)
        real[0, LANES * h + FOX_HEAD * e:LANES * h + FOX_HEAD * (e + 1)] = 1.0
        for x in range(3):
            pq[LANES * x + h, free + x] = 1.0
            pk[LANES * x + h, free + 3 + x] = -1.0
            oq[0, free + 3 + x] = 1.0
            ok[0, free + x] = 1.0
    return (jnp.asarray(real), (jnp.asarray(pq, BF16), jnp.asarray(oq)), (jnp.asarray(pk, BF16), jnp.asarray(ok)))


def _head_slots(x, real, bias):
    pairs = x.shape[1] // LANES
    dup = jnp.concatenate([x[:, LANES * p:LANES * (p + 1)] for p in range(pairs) for _ in range(2)], axis=1)
    return jnp.where(real > 0.0, dup, bias).astype(BF16)


def _bias_slots(c, p_ref, ones_ref):
    hi, mid, lo = _split3(c * LOG2E)
    return _dot(jnp.concatenate([hi, mid, lo], axis=1), p_ref[...]) + ones_ref[...]


def _shared_kv_kernel(x_ref, g_ref, wk_ref, wv_ref, wf_ref, bf_ref, kn_ref, bd_ref, tri_ref, *rest, aug):
    if aug:
        real_ref, pk_ref, ok_ref, k_ref, v_ref, lf_ref, c_ref, ka_ref, va_ref, carry = rest
    else:
        k_ref, v_ref, lf_ref, c_ref, carry = rest
    j = pl.program_id(1)
    heads = lf_ref.shape[1]

    @pl.when(j == 0)
    def _():
        carry[...] = jnp.zeros_like(carry)

    hn = _rms(x_ref[...], g_ref[...]).astype(BF16)
    k = _group_rms(_dot(hn, wk_ref[...]), bd_ref, FOX_HEAD) * kn_ref[...]
    v = _dot(hn, wv_ref[...])
    lf = _log_sigmoid(_dot(hn, wf_ref[...]) + bf_ref[...])
    c = _dot3_right(tri_ref[...], lf) + carry[...]
    carry[...] = c[c.shape[0] - 1:, :]
    lf_ref[...] = lf[:, :heads]
    c_ref[...] = c
    if aug:
        tm = k.shape[0]
        k_ref[0] = k.T.reshape(heads, FOX_HEAD, tm)
        v_ref[0] = v.T.reshape(heads, FOX_HEAD, tm)
        real = real_ref[...]
        ka_ref[...] = _head_slots(k, real, _bias_slots(c, pk_ref, ok_ref))
        va_ref[...] = _head_slots(v, real, 1.0)
    else:
        k_ref[...] = k
        v_ref[...] = v


def _shared_kv(x, g, wk, wv, wf, bf, kn, bd, tri, groups, heads, aug=None):
    n, d = x.shape
    tm = tri.shape[0]
    nj = n // groups // tm
    row = lambda w: pl.BlockSpec((tm, w), lambda b, j: (b * nj + j, 0))
    consts = [g, wk, wv, wf, bf, kn, bd, tri]
    tail_specs = [row(heads), row(LANES)]
    tail_shape = [jax.ShapeDtypeStruct((n, heads), F32), jax.ShapeDtypeStruct((n, LANES), F32)]
    if aug is None:
        extra = []
        out_specs = [row(d), row(d)] + tail_specs
        out_shape = [jax.ShapeDtypeStruct((n, d), F32)] * 2 + tail_shape
    else:
        real, _, (pk, ok) = aug
        extra = [real, pk, ok]
        kv_t = pl.BlockSpec((1, heads, FOX_HEAD, tm), lambda b, j: (b, 0, 0, j))
        out_specs = [kv_t, kv_t] + tail_specs + [row(heads * LANES)] * 2
        out_shape = ([jax.ShapeDtypeStruct((groups, heads, FOX_HEAD, nj * tm), F32)] * 2 + tail_shape
                     + [jax.ShapeDtypeStruct((n, heads * LANES), BF16)] * 2)
    return pl.pallas_call(
        functools.partial(_shared_kv_kernel, aug=aug is not None),
        grid=(groups, nj),
        in_specs=[row(d)] + [_const_spec(a) for a in consts + extra],
        out_specs=out_specs,
        out_shape=out_shape,
        scratch_shapes=[pltpu.VMEM((1, LANES), F32)],
        compiler_params=_params(("arbitrary", "arbitrary")),
    )(x, *consts, *extra)


def _fox_qg_kernel(x_ref, g_ref, w_ref, qn_ref, bd_ref, *rest, aug):
    d = x_ref.shape[1]
    qg = _dot(_rms(x_ref[...], g_ref[...]).astype(BF16), w_ref[...])
    q = _group_rms(qg[:, :d], bd_ref, FOX_HEAD) * (qn_ref[...] * FOX_HEAD ** -0.5)
    if aug:
        c_ref, real_ref, pq_ref, oq_ref, q_ref, gt_ref = rest
        q_ref[...] = _head_slots(q * LOG2E, real_ref[...], _bias_slots(c_ref[...], pq_ref, oq_ref))
    else:
        q_ref, gt_ref = rest
        q_ref[...] = q
    gt_ref[...] = qg[:, d:].astype(BF16)


def _fox_qg(x, g, w, qn, bd, c=None, aug=None):
    n, d = x.shape
    tm = min(ROW_TILE, n)
    row = lambda w_: pl.BlockSpec((tm, w_), lambda i: (i, 0))
    full = lambda a: pl.BlockSpec(a.shape, lambda i: (0,) * a.ndim)
    consts = [g, w, qn, bd]
    if aug is None:
        extra, extra_specs = [], []
        q_spec, q_shape = row(d), jax.ShapeDtypeStruct((n, d), F32)
    else:
        real, (pq, oq), _ = aug
        extra, extra_specs = [c, real, pq, oq], [row(LANES), full(real), full(pq), full(oq)]
        width = real.shape[1]
        q_spec, q_shape = row(width), jax.ShapeDtypeStruct((n, width), BF16)
    return pl.pallas_call(
        functools.partial(_fox_qg_kernel, aug=aug is not None),
        grid=(n // tm,),
        in_specs=[row(d)] + [_const_spec(a) for a in consts] + extra_specs,
        out_specs=[q_spec, row(d)],
        out_shape=[q_shape, jax.ShapeDtypeStruct((n, d), BF16)],
        compiler_params=_params(("arbitrary",)),
    )(x, *[_arr(a) for a in consts], *extra)


def _attn_prompt_kernel(q_ref, k_ref, v_ref, o_ref):
    i = pl.program_id(2)
    tq = q_ref.shape[0]

    def slot(ref, rows, e):
        return ref[rows, e * LANES:(e + 1) * LANES]

    def block(first_tile, tiles, state, keep):
        rows = pl.ds(pl.multiple_of(first_tile * tq, tq), tiles * tq)
        out = []
        for e in range(2):
            m, acc = state[2 * e:2 * e + 2]
            s = _dot_nt(slot(q_ref, slice(None), e), slot(k_ref, rows, e))
            if keep is not None:
                s = jnp.where(keep, s, -jnp.inf)
            m_new = jnp.maximum(m, jnp.max(s, axis=-1, keepdims=True))
            p = jnp.exp2(s - m_new)
            out += [m_new, jnp.exp2(m - m_new) * acc + _dot(p.astype(BF16), slot(v_ref, rows, e))]
        return tuple(out)

    init = (jnp.full((tq, 1), -jnp.inf, F32), jnp.zeros((tq, LANES), F32)) * 2
    state = lax.fori_loop(0, i // 2, lambda j, st: block(2 * j, 2, st, None), init)

    def tail(tiles):
        def run(st):
            col = lax.broadcasted_iota(jnp.int32, (tq, tiles * tq), 1)
            row = lax.broadcasted_iota(jnp.int32, (tq, tiles * tq), 0)
            return block(i - (tiles - 1), tiles, st, col <= row + (tiles - 1) * tq)
        return run

    state = lax.cond(i % 2 == 1, tail(2), tail(1), state)
    outs = [acc / pltpu.roll(acc, FOX_HEAD, axis=1) for acc in (state[1], state[3])]
    lane = lax.broadcasted_iota(jnp.int32, (tq, LANES), 1)
    o_ref[...] = jnp.where(lane < FOX_HEAD, outs[0], outs[1]).astype(o_ref.dtype)


def _attn_prompt(qa, ka, va, batch):
    n, width = qa.shape
    t = n // batch
    tq = min(ATTN_TILE, t)
    nq = t // tq
    pairs = width // (2 * LANES)
    return pl.pallas_call(
        _attn_prompt_kernel,
        grid=(batch, pairs, nq),
        in_specs=[pl.BlockSpec((tq, 2 * LANES), lambda b, p, i: (b * nq + i, p)),
                  pl.BlockSpec((t, 2 * LANES), lambda b, p, i: (b, p)),
                  pl.BlockSpec((t, 2 * LANES), lambda b, p, i: (b, p))],
        out_specs=pl.BlockSpec((tq, LANES), lambda b, p, i: (b * nq + i, p)),
        out_shape=jax.ShapeDtypeStruct((n, pairs * LANES), BF16),
        compiler_params=_params(("arbitrary", "arbitrary", "arbitrary")),
    )(qa, ka, va)


def _page_prefix(n_pages, heads):
    r = np.arange(n_pages * heads)
    return jnp.asarray(((r[:, None] % heads) == (r[None, :] % heads)) & ((r[None, :] // heads) < (r[:, None] // heads)),
                       BF16)


def _attn_decode_kernel(pt_ref, q_ref, *refs, n_pages):
    del pt_ref
    kp, vp, lp = refs[:n_pages], refs[n_pages:2 * n_pages], refs[2 * n_pages:3 * n_pages]
    kn_ref, vn_ref, cn_ref, cq_ref, hm_ref, u_ref, px_ref, o_ref = refs[3 * n_pages:]
    seq, d = q_ref.shape[1:]
    hm = hm_ref[...]
    heads = hm.shape[0]
    rows = seq * heads
    q = (q_ref[0][:, None, :] * hm[None, :, :]).reshape(rows, d).astype(BF16)

    c_page = _dot3_left(jnp.concatenate([lp[g][0] for g in range(n_pages)], axis=0), u_ref[...])
    page_total = jnp.broadcast_to(c_page[:, PAGE - 1:PAGE], c_page.shape)
    before = _dot3_right(px_ref[...], page_total)
    c_past = c_page + before
    total = (before + page_total)[(n_pages - 1) * heads:, :]
    cq = cq_ref[0] + jnp.concatenate([total] * seq, axis=0)

    def update(state, s, ck, keep, pv):
        m, l, acc = state
        s = s + (jnp.concatenate([cq] * (s.shape[1] // PAGE), axis=1) - jnp.concatenate([ck] * seq, axis=0))
        if keep is not None:
            s = jnp.where(keep, s, -jnp.inf)
        m_new = jnp.maximum(m, jnp.max(s, axis=-1, keepdims=True))
        pr = jnp.exp(s - m_new)
        alpha = jnp.exp(m - m_new)
        return m_new, alpha * l + jnp.sum(pr, axis=-1, keepdims=True), alpha * acc + pv(pr.astype(BF16))

    state = (jnp.full((rows, 1), -jnp.inf, F32), jnp.zeros((rows, 1), F32), jnp.zeros((rows, d), F32))
    for g0 in range(0, n_pages, DECODE_PAGES):
        group = range(g0, g0 + DECODE_PAGES)
        k_cat = jnp.concatenate([kp[g][0].astype(BF16) for g in group], axis=1)
        v_cat = jnp.concatenate([vp[g][0].astype(BF16) for g in group], axis=1)
        ck = jnp.concatenate([c_past[g * heads:(g + 1) * heads, :] for g in group], axis=1)
        state = update(state, _dot(q, k_cat), ck, None, lambda pr, v_cat=v_cat: _dot_nt(pr, v_cat))

    pad = jnp.zeros((PAGE - seq, d), F32)
    k_new = jnp.concatenate([kn_ref[0], pad], axis=0).astype(BF16)
    v_new = jnp.concatenate([vn_ref[0], pad], axis=0).astype(BF16)
    t = lax.broadcasted_iota(jnp.int32, (rows, PAGE), 0) // heads
    key = lax.broadcasted_iota(jnp.int32, (rows, PAGE), 1)
    _, l, acc = update(state, _dot_nt(q, k_new), cn_ref[0] + total, key <= t, lambda pr: _dot(pr, v_new))
    o_ref[0] = jnp.sum((acc / l).reshape(seq, heads, d) * hm[None, :, :], axis=1)


def _attn_decode(q, cache_kt, cache_vt, cache_lt, k_new, v_new, c_new_t, cq_col, head_mask, upper, prefix,
                 page_table):
    bd, seq, d = q.shape
    n_pages = page_table.shape[1]
    heads = head_mask.shape[0]
    page = lambda g, w: pl.BlockSpec((1, w, PAGE), lambda s, pt: (pt[s * n_pages + g], 0, 0))
    tok = pl.BlockSpec((1, seq, d), lambda s, pt: (s, 0, 0))
    grid_spec = pltpu.PrefetchScalarGridSpec(
        num_scalar_prefetch=1,
        grid=(bd,),
        in_specs=[tok] + [page(g, d) for g in range(n_pages)] + [page(g, d) for g in range(n_pages)]
        + [page(g, heads) for g in range(n_pages)] + [
            tok, tok,
            pl.BlockSpec((1, heads, PAGE), lambda s, pt: (s, 0, 0)),
            pl.BlockSpec((1, seq * heads, 1), lambda s, pt: (s, 0, 0)),
            _const_spec(head_mask), _const_spec(upper), _const_spec(prefix)],
        out_specs=tok,
    )
    return pl.pallas_call(
        functools.partial(_attn_decode_kernel, n_pages=n_pages),
        grid_spec=grid_spec,
        out_shape=jax.ShapeDtypeStruct((bd, seq, d), F32),
        compiler_params=_params(("arbitrary",)),
    )(page_table.reshape(-1), q, *([cache_kt] * n_pages), *([cache_vt] * n_pages), *([cache_lt] * n_pages),
      k_new, v_new, c_new_t, cq_col, head_mask, upper, prefix)


def _trunk(x, seq, hgrn_s0, conv_s0, past, w):
    n, d = x.shape
    n_seq = n // seq
    depth = w["ffn_w_in"].shape[0]
    n_a = w["hgrn_w_in"].shape[0]
    heads = w["fox_b_f"].shape[0]
    prompt = past is None
    h = x
    new_hgrn, new_conv = [], []
    for layer in range(depth):
        if layer < n_a:
            q, k, lf, i, gt = _hgrn_in(h, (w["norm_mix"], layer), (w["hgrn_w_in"], layer), w["hgrn_lb_param"], layer)
            if prompt:
                o, s = _hgrn_prompt(q, k, lf, i, n_seq, w["hgrn_prompt_consts"])
            else:
                o, s = _hgrn_decode(q, k, lf, i, hgrn_s0, layer, seq, w["hgrn_decode_consts"])
            new_hgrn.append(s)
            mix = ("hgrn", o, gt, h, [(w["hgrn_onorm"], layer), (w["hgrn_w_o"], layer)])
        else:
            j = layer - n_a
            qg = (h, (w["norm_mix"], layer), (w["fox_w_qg"], j), (w["q_norm"], j), w["bd_fox"])
            if prompt:
                qa, gt = _fox_qg(*qg, c, w["aug"])
                o = _attn_prompt(qa, ka, va, n_seq)
            else:
                q, gt = _fox_qg(*qg)
                o = _attn_decode(q.reshape(n_seq, seq, d), *past[:3], k_sh.reshape(n_seq, seq, d),
                                 v_sh.reshape(n_seq, seq, d), c_new_t, cq_col, w["head_mask"], w["upper"],
                                 w["page_prefix"], past[3]).reshape(n, d)
            mix = ("fox", o, gt, h, [(w["fox_w_o"], j)])
        ffn = [(w[name], layer) for name in ("norm_ffn", "ffn_w_in", "ffn_conv_w", "ffn_conv_b", "ffn_w_out")]
        if prompt:
            h, cs = _ffn_prompt(*mix, ffn, n_seq)
        else:
            h, cs = _ffn_decode(*mix, ffn, conv_s0, layer, seq)
        new_conv.append(cs)
        if layer == n_a - 1:
            tri = w["tri_prompt"] if prompt else w["tri_decode"]
            kv = _shared_kv(h, w["kv_norm"], w["w_k"], w["w_v"], w["w_f"], w["b_f"], w["k_norm"], w["bd_fox"], tri,
                            n_seq if prompt else n // tri.shape[0], heads, w["aug"] if prompt else None)
            k_sh, v_sh, logf, c = kv[:4]
            if prompt:
                ka, va = kv[4:]
            else:
                c3 = c[:, :heads].reshape(n_seq, seq, heads)
                cq_col = c3.reshape(n_seq, seq * heads, 1)
                c_new_t = jnp.pad(c3.transpose(0, 2, 1), ((0, 0), (0, 0), (0, PAGE - seq)))
    return h, k_sh, v_sh, logf, jnp.stack(new_hgrn), jnp.stack(new_conv)


def kernel(x_prompt, x_sample, cache_k, cache_v, cache_logf, state_hgrn, state_conv, page_table, norm_mix, norm_ffn, hgrn_w_in, hgrn_lb_param, hgrn_onorm, hgrn_w_o, kv_norm, w_kvf, fox_b_f, k_norm, fox_w_qg, q_norm, fox_w_o, ffn_w_in, ffn_conv_w, ffn_conv_b, ffn_w_out):
    b, t, d = x_prompt.shape
    bd, ts, _ = x_sample.shape
    heads = fox_b_f.shape[0]
    n_phys, page = cache_k.shape[:2]
    assert page == PAGE and d % LANES == 0 and d // heads == FOX_HEAD
    assert t % ATTN_TILE == 0 and page_table.shape[1] % DECODE_PAGES == 0
    assert t % ROW_TILE == 0 and (bd * ts) % ROW_TILE == 0 and CHUNK % ts == 0 and ROW_TILE % ts == 0
    assert ts >= CONV_W - 1

    row = lambda a: a.reshape(a.shape[:-1] + (1, a.shape[-1]))
    w = {
        "norm_mix": row(norm_mix), "norm_ffn": row(norm_ffn),
        "hgrn_w_in": hgrn_w_in.astype(BF16), "hgrn_lb_param": hgrn_lb_param,
        "hgrn_onorm": row(jnp.tile(hgrn_onorm, (1, d // HGRN_HEAD))), "hgrn_w_o": hgrn_w_o.astype(BF16),
        "kv_norm": row(kv_norm),
        "w_k": w_kvf[:, :d].astype(BF16), "w_v": w_kvf[:, d:2 * d].astype(BF16),
        "w_f": jnp.pad(w_kvf[:, 2 * d:], ((0, 0), (0, LANES - heads))).astype(BF16),
        "b_f": jnp.pad(fox_b_f, (0, LANES - heads)).reshape(1, LANES), "fox_b_f": fox_b_f,
        "k_norm": jnp.tile(k_norm, heads).reshape(1, d),
        "fox_w_qg": fox_w_qg.astype(BF16), "q_norm": row(jnp.tile(q_norm, (1, heads))),
        "fox_w_o": fox_w_o.astype(BF16),
        "ffn_w_in": ffn_w_in.astype(BF16), "ffn_conv_w": ffn_conv_w, "ffn_conv_b": row(ffn_conv_b),
        "ffn_w_out": ffn_w_out.astype(BF16),
        "hgrn_prompt_consts": _level_mats(CHUNK, CHUNK), "hgrn_decode_consts": _level_mats(CHUNK, ts),
        "bd_fox": _block_diag_ones(d, FOX_HEAD),
        "tri_prompt": _seg_lower_tri(ROW_TILE, ROW_TILE), "tri_decode": _seg_lower_tri(ROW_TILE, ts),
        "upper": _upper_tri(PAGE), "head_mask": _head_mask(heads, FOX_HEAD),
        "aug": _aug_consts(heads), "page_prefix": _page_prefix(page_table.shape[1], heads),
    }

    y_p, k_p, v_p, lf_p, hgrn_p, conv_p = _trunk(x_prompt.reshape(b * t, d), t, None, None, None, w)

    past = (cache_k.transpose(0, 2, 3, 1).reshape(n_phys, d, page), cache_v.transpose(0, 2, 3, 1).reshape(n_phys, d, page),
            cache_logf.astype(F32).transpose(0, 2, 1), page_table)
    y_s, k_s, v_s, lf_s, hgrn_s, conv_s = _trunk(x_sample.reshape(bd * ts, d), ts, state_hgrn, state_conv, past, w)

    hd = (heads, d // heads)
    return (y_p.reshape(b, t, d), y_s.reshape(bd, ts, d),
            k_p.transpose(0, 3, 1, 2), v_p.transpose(0, 3, 1, 2), lf_p.reshape(b, t, heads),
            k_s.reshape((bd, ts) + hd), v_s.reshape((bd, ts) + hd), lf_s.reshape(bd, ts, heads),
            hgrn_p, hgrn_s, conv_p, conv_s)
```

```python
import functools

import numpy as np
import jax
import jax.numpy as jnp
from jax import lax
from jax.experimental import pallas as pl
from jax.experimental.pallas import tpu as pltpu

F32 = jnp.float32
BF16 = jnp.bfloat16
EPS = 1e-6

LANES = 128
SUBLANES = 8
VMEM_LIMIT_BYTES = 56 * 1024 * 1024

HGRN_HEAD = 128
FOX_HEAD = 64
CONV_W = 3
CHUNK = 128
ROW_TILE = 512
ATTN_TILE = 512
PAGE = 128
DECODE_PAGES = 4
LOG2E = 1.4426950408889634


def _params(semantics):
    return pltpu.CompilerParams(dimension_semantics=semantics, vmem_limit_bytes=VMEM_LIMIT_BYTES)


def _const_spec(a):
    if isinstance(a, tuple):
        arr, layer = a
        return pl.BlockSpec((None,) + arr.shape[1:], lambda *_: (layer,) + (0,) * (arr.ndim - 1))
    return pl.BlockSpec(a.shape, lambda *_: (0,) * a.ndim)


def _arr(a):
    return a[0] if isinstance(a, tuple) else a


def _dot(a, b):
    return jnp.dot(a, b, preferred_element_type=F32)


def _dot_nt(a, b):
    return lax.dot_general(a, b, (((1,), (1,)), ((), ())), preferred_element_type=F32)


def _split3(x):
    hi = x.astype(BF16)
    r = x - hi.astype(F32)
    mid = r.astype(BF16)
    lo = (r - mid.astype(F32)).astype(BF16)
    return hi, mid, lo


def _dot3_right(t, x):
    hi, mid, lo = _split3(x)
    return _dot(t, hi) + _dot(t, mid) + _dot(t, lo)


def _dot3_left(x, t):
    hi, mid, lo = _split3(x)
    return _dot(hi, t) + _dot(mid, t) + _dot(lo, t)


def _rms(x, g):
    r = lax.rsqrt(jnp.mean(x * x, axis=-1, keepdims=True) + EPS)
    return x * r * g


def _sigmoid(x):
    return 1.0 / (1.0 + jnp.exp(-x))


def _log_sigmoid(x):
    return jnp.minimum(x, 0.0) - jnp.log(1.0 + jnp.exp(-jnp.abs(x)))


def _group_rms(x, bd_ref, width):
    ss = _dot((x * x).astype(BF16), bd_ref[...])
    return x * lax.rsqrt(ss * (1.0 / width) + EPS)


def _level_mats(rows, seg):
    t = np.arange(rows)[:, None]
    j = np.arange(rows)[None, :]
    mats = [((t // seg) == (j // seg)) & (j <= t)]
    lv = np.full((rows, rows), -1, np.int32)
    lv[np.arange(rows), np.arange(rows)] = 0
    h, level = seg // 2, 1
    while h >= 1:
        blk = t // (2 * h)
        m = blk * 2 * h + h
        in_blk = (j // (2 * h)) == blk
        upper = t >= m
        if h < SUBLANES:
            mats.append(in_blk & np.where(upper, (j >= m) & (j <= t), (j >= t + 1) & (j <= m - 1)))
        lv[in_blk & upper & ((j % (2 * h)) < h)] = level
        h //= 2
        level += 1
    tall = np.concatenate([m.astype(np.float32) for m in mats], axis=0)
    return jnp.asarray(tall, BF16), jnp.asarray(lv), seg


def _block_diag_ones(n, width):
    i = np.arange(n)
    return jnp.asarray((i[:, None] // width) == (i[None, :] // width), BF16)


def _seg_lower_tri(n, seg):
    i = np.arange(n)
    return jnp.asarray(((i[:, None] // seg) == (i[None, :] // seg)) & (i[None, :] <= i[:, None]), BF16)


def _upper_tri(n):
    i = np.arange(n)
    return jnp.asarray(i[:, None] <= i[None, :], BF16)


def _head_mask(heads, width):
    lane = np.arange(heads * width)
    return jnp.asarray((lane[None, :] // width) == np.arange(heads)[:, None], F32)


def _hgrn_in_kernel(x_ref, g_ref, w_ref, lbp_ref, q_ref, k_ref, lf_ref, i_ref, gt_ref, *, layer):
    d = q_ref.shape[1]
    hn = _rms(x_ref[...], g_ref[...]).astype(BF16)
    proj = _dot(hn, w_ref[...])
    q = proj[:, :d]
    fr = proj[:, d:2 * d]
    q_ref[...] = (q * _sigmoid(q)).astype(BF16)
    i_ref[...] = proj[:, 2 * d:3 * d].astype(BF16)
    gt_ref[...] = proj[:, 3 * d:].astype(BF16)
    if layer == 0:
        lf_ref[...] = _log_sigmoid(fr)
        k_ref[...] = _sigmoid(-fr).astype(BF16)
    else:
        lbp = lbp_ref[...]
        e = jnp.exp(lbp - jnp.max(lbp, axis=0, keepdims=True))
        p = e / jnp.sum(e, axis=0, keepdims=True)
        lb = jnp.sum(p[1:layer + 1], axis=0, keepdims=True)
        lf_ref[...] = jnp.log(lb + (1.0 - lb) * _sigmoid(fr))
        k_ref[...] = ((1.0 - lb) * _sigmoid(-fr)).astype(BF16)


def _hgrn_in(x, g, w, lbp, layer):
    n, d = x.shape
    tm = min(ROW_TILE, n)
    row = pl.BlockSpec((tm, d), lambda i: (i, 0))
    out = lambda dtype: jax.ShapeDtypeStruct((n, d), dtype)
    return pl.pallas_call(
        functools.partial(_hgrn_in_kernel, layer=layer),
        grid=(n // tm,),
        in_specs=[row, _const_spec(g), _const_spec(w), _const_spec(lbp)],
        out_specs=[row] * 5,
        out_shape=[out(BF16), out(BF16), out(F32), out(BF16), out(BF16)],
        compiler_params=_params(("arbitrary",)),
    )(x, _arr(g), _arr(w), lbp)


def _hgrn_intra(q, k, lf, v_bf, tall_ref, lv, seg):
    c, width = q.shape
    hi, mid, lo = _split3(lf)
    d3 = _dot(tall_ref[...], jnp.concatenate([hi, mid, lo], axis=1))
    d = d3[:, :c] + d3[:, c:2 * c] + d3[:, 2 * c:]
    b = d[:c]
    a = jnp.where(lv == 0, _dot_nt(q.astype(BF16), k.astype(BF16)), 0.0)
    h, level, blk = seg // 2, 1, 1
    while h >= 1:
        if h >= SUBLANES:
            bh = b.reshape(c // (2 * h), 2 * h, width)
            e = jnp.exp(-jnp.abs(bh - bh[:, h - 1:h, :])).reshape(c, width)
        else:
            e = jnp.exp(d[blk * c:(blk + 1) * c])
            blk += 1
        a = jnp.where(lv == level, _dot_nt((q * e).astype(BF16), (k * e).astype(BF16)), a)
        h //= 2
        level += 1
    bs = b.reshape(c // seg, seg, width)
    return _dot(a.astype(BF16), v_bf), b, (bs[:, seg - 1:seg, :] - bs).reshape(c, width)


def _hgrn_prompt_kernel(q_ref, k_ref, lf_ref, v_ref, tall_ref, lv_ref, o_ref, sout_ref, s_scr):
    j = pl.program_id(2)

    @pl.when(j == 0)
    def _():
        s_scr[...] = jnp.zeros_like(s_scr)

    lv = lv_ref[...]
    s = s_scr[...]
    for c in range(q_ref.shape[0] // CHUNK):
        rows = pl.ds(c * CHUNK, CHUNK)
        q, k, lf, v = q_ref[rows, :], k_ref[rows, :], lf_ref[rows, :], v_ref[rows, :]
        v_bf = v.astype(BF16)
        o_intra, b, rev = _hgrn_intra(q, k, lf, v_bf, tall_ref, lv, CHUNK)
        eb = jnp.exp(b)
        o_ref[rows, :] = (o_intra + _dot((q * eb).astype(BF16), s.astype(BF16))).astype(o_ref.dtype)
        ks_t = (k * jnp.exp(rev)).T.astype(BF16)
        decay = eb.T[:, CHUNK - 1:CHUNK]
        s = s * decay + _dot(ks_t, v_bf)
    s_scr[...] = s

    @pl.when(j == pl.num_programs(2) - 1)
    def _():
        sout_ref[0, 0] = s


def _hgrn_prompt(q, k, lf, v, batch, consts):
    tall, lv, _ = consts
    n, d = q.shape
    heads = d // HGRN_HEAD
    t = n // batch
    tb = min(8 * CHUNK, t)
    nj = t // tb
    blk = pl.BlockSpec((tb, HGRN_HEAD), lambda b, h, j: (b * nj + j, h))
    full = lambda a: pl.BlockSpec(a.shape, lambda b, h, j: (0,) * a.ndim)
    return pl.pallas_call(
        _hgrn_prompt_kernel,
        grid=(batch, heads, nj),
        in_specs=[blk, blk, blk, blk, full(tall), full(lv)],
        out_specs=[blk, pl.BlockSpec((1, 1, HGRN_HEAD, HGRN_HEAD), lambda b, h, j: (b, h, 0, 0))],
        out_shape=[jax.ShapeDtypeStruct((n, d), BF16),
                   jax.ShapeDtypeStruct((batch, heads, HGRN_HEAD, HGRN_HEAD), F32)],
        scratch_shapes=[pltpu.VMEM((HGRN_HEAD, HGRN_HEAD), F32)],
        compiler_params=_params(("arbitrary", "arbitrary", "arbitrary")),
    )(q, k, lf, v, tall, lv)


def _hgrn_decode_kernel(q_ref, k_ref, lf_ref, v_ref, s0_ref, tall_ref, lv_ref, o_ref, sout_ref, *, seq):
    q, k, lf, v = q_ref[...], k_ref[...], lf_ref[...], v_ref[...]
    v_bf = v.astype(BF16)
    o_intra, b, rev = _hgrn_intra(q, k, lf, v_bf, tall_ref, lv_ref[...], seq)
    eb = jnp.exp(b)
    qs = q * eb
    ks_t = (k * jnp.exp(rev)).T
    eb_t = eb.T
    lane = lax.broadcasted_iota(jnp.int32, ks_t.shape, 1)
    row = lax.broadcasted_iota(jnp.int32, qs.shape, 0)
    o = o_intra
    for i in range(CHUNK // seq):
        s0 = s0_ref[i, 0]
        mine = (lane >= i * seq) & (lane < (i + 1) * seq)
        o_i = _dot(qs.astype(BF16), s0.astype(BF16))
        o = o + jnp.where((row >= i * seq) & (row < (i + 1) * seq), o_i, 0.0)
        ds = _dot(jnp.where(mine, ks_t, 0.0).astype(BF16), v_bf)
        sout_ref[i, 0] = s0 * eb_t[:, (i + 1) * seq - 1:(i + 1) * seq] + ds
    o_ref[...] = o.astype(o_ref.dtype)


def _hgrn_decode(q, k, lf, v, s0_all, layer, seq, consts):
    tall, lv, _ = consts
    n, d = q.shape
    heads = d // HGRN_HEAD
    per = CHUNK // seq
    blk = pl.BlockSpec((CHUNK, HGRN_HEAD), lambda g, h: (g, h))
    sblk = pl.BlockSpec((per, 1, HGRN_HEAD, HGRN_HEAD), lambda g, h: (g, h, 0, 0))
    s0blk = pl.BlockSpec((None, per, 1, HGRN_HEAD, HGRN_HEAD), lambda g, h: (layer, g, h, 0, 0))
    return pl.pallas_call(
        functools.partial(_hgrn_decode_kernel, seq=seq),
        grid=(n // CHUNK, heads),
        in_specs=[blk, blk, blk, blk, s0blk, _const_spec(tall), _const_spec(lv)],
        out_specs=[blk, sblk],
        out_shape=[jax.ShapeDtypeStruct((n, d), BF16), jax.ShapeDtypeStruct(s0_all.shape[1:], F32)],
        compiler_params=_params(("arbitrary", "arbitrary")),
    )(q, k, lf, v, s0_all, tall, lv)


_MIX_CONSTS = {"hgrn": 2, "fox": 1}


def _mix_residual(kind, o_ref, gt_ref, x_ref, consts):
    o = o_ref[...].astype(F32)
    gt = gt_ref[...].astype(F32)
    if kind == "hgrn":
        on_ref, w_ref = consts
        parts = []
        for h in range(o.shape[1] // HGRN_HEAD):
            oh = o[:, h * HGRN_HEAD:(h + 1) * HGRN_HEAD]
            parts.append(oh * lax.rsqrt(jnp.mean(oh * oh, axis=-1, keepdims=True) + EPS))
        y = jnp.concatenate(parts, axis=-1) * on_ref[...] * (gt * _sigmoid(gt))
    else:
        (w_ref,) = consts
        y = o * _sigmoid(gt)
    return x_ref[...] + _dot(y.astype(BF16), w_ref[...])


def _ffn_tail(h, u, u1, u2, gate, cw_ref, cb_ref, wout_ref):
    a = cb_ref[...] + cw_ref[0:1, :] * u2 + cw_ref[1:2, :] * u1 + cw_ref[2:3, :] * u
    y = a * _sigmoid(a) * gate
    return h + _dot(y.astype(BF16), wout_ref[...])


def _ffn_prompt_kernel(o_ref, gt_ref, x_ref, *refs, kind):
    mix_consts = refs[:_MIX_CONSTS[kind]]
    g_ref, win_ref, cw_ref, cb_ref, wout_ref, h_ref, cs_ref, carry = refs[_MIX_CONSTS[kind]:]
    j = pl.program_id(1)
    f = cw_ref.shape[1]

    @pl.when(j == 0)
    def _():
        carry[...] = jnp.zeros_like(carry)

    h = _mix_residual(kind, o_ref, gt_ref, x_ref, mix_consts)
    tm = h.shape[0]
    ug = _dot(_rms(h, g_ref[...]).astype(BF16), win_ref[...])
    u = ug[:, :f]
    prev = carry[...]
    p0 = prev[SUBLANES - 2:SUBLANES - 1, :]
    p1 = prev[SUBLANES - 1:SUBLANES, :]
    row = lax.broadcasted_iota(jnp.int32, u.shape, 0)
    u1 = jnp.where(row == 0, p1, pltpu.roll(u, 1, axis=0))
    u2 = jnp.where(row == 0, p0, jnp.where(row == 1, p1, pltpu.roll(u, 2, axis=0)))
    h_ref[...] = _ffn_tail(h, u, u1, u2, ug[:, f:], cw_ref, cb_ref, wout_ref)
    carry[...] = u[tm - SUBLANES:, :]

    @pl.when(j == pl.num_programs(1) - 1)
    def _():
        cs_ref[0] = u[tm - (CONV_W - 1):, :]


def _ffn_prompt(kind, o, gt, x, mix_consts, ffn_consts, batch):
    n, d = x.shape
    f = _arr(ffn_consts[2]).shape[-1]
    t = n // batch
    tm = min(ROW_TILE, t)
    nj = t // tm
    row = pl.BlockSpec((tm, d), lambda b, j: (b * nj + j, 0))
    consts = list(mix_consts) + list(ffn_consts)
    return pl.pallas_call(
        functools.partial(_ffn_prompt_kernel, kind=kind),
        grid=(batch, nj),
        in_specs=[row, row, row] + [_const_spec(c) for c in consts],
        out_specs=[row, pl.BlockSpec((1, CONV_W - 1, f), lambda b, j: (b, 0, 0))],
        out_shape=[jax.ShapeDtypeStruct((n, d), F32), jax.ShapeDtypeStruct((batch, CONV_W - 1, f), F32)],
        scratch_shapes=[pltpu.VMEM((SUBLANES, f), F32)],
        compiler_params=_params(("arbitrary", "arbitrary")),
    )(o, gt, x, *[_arr(c) for c in consts])


def _ffn_decode_kernel(o_ref, gt_ref, x_ref, prev_ref, *refs, kind, seq):
    mix_consts = refs[:_MIX_CONSTS[kind]]
    g_ref, win_ref, cw_ref, cb_ref, wout_ref, h_ref, cs_ref = refs[_MIX_CONSTS[kind]:]
    f = cw_ref.shape[1]
    h = _mix_residual(kind, o_ref, gt_ref, x_ref, mix_consts)
    tm = h.shape[0]
    ug = _dot(_rms(h, g_ref[...]).astype(BF16), win_ref[...])
    u = ug[:, :f]
    u3 = u.reshape(tm // seq, seq, f)
    prev = prev_ref[...]
    p0 = prev[:, 0:1, :]
    p1 = prev[:, 1:2, :]
    row = lax.broadcasted_iota(jnp.int32, u3.shape, 1)
    u1 = jnp.where(row == 0, p1, pltpu.roll(u3, 1, axis=1))
    u2 = jnp.where(row == 0, p0, jnp.where(row == 1, p1, pltpu.roll(u3, 2, axis=1)))
    h_ref[...] = _ffn_tail(h, u, u1.reshape(tm, f), u2.reshape(tm, f), ug[:, f:], cw_ref, cb_ref, wout_ref)
    cs_ref[...] = u3[:, seq - (CONV_W - 1):, :]


def _ffn_decode(kind, o, gt, x, mix_consts, ffn_consts, prev_all, layer, seq):
    n, d = x.shape
    f = _arr(ffn_consts[2]).shape[-1]
    tm = min(ROW_TILE, n)
    row = pl.BlockSpec((tm, d), lambda i: (i, 0))
    st = pl.BlockSpec((tm // seq, CONV_W - 1, f), lambda i: (i, 0, 0))
    st_in = pl.BlockSpec((None, tm // seq, CONV_W - 1, f), lambda i: (layer, i, 0, 0))
    consts = list(mix_consts) + list(ffn_consts)
    return pl.pallas_call(
        functools.partial(_ffn_decode_kernel, kind=kind, seq=seq),
        grid=(n // tm,),
        in_specs=[row, row, row, st_in] + [_const_spec(c) for c in consts],
        out_specs=[row, st],
        out_shape=[jax.ShapeDtypeStruct((n, d), F32), jax.ShapeDtypeStruct(prev_all.shape[1:], F32)],
        compiler_params=_params(("arbitrary",)),
    )(o, gt, x, prev_all, *[_arr(c) for c in consts])


def _aug_consts(heads):
    width = heads * LANES
    pq, pk = np.zeros((3 * LANES, width), np.float32), np.zeros((3 * LANES, width), np.float32)
    oq, ok, real = (np.zeros((1, width), np.float32) for _ in range(3))
    for h in range(heads):
        e = h % 2
        free = LANES * h + FOX_HEAD * (1 - e)
        real[0, LANES * h + FOX_HEAD * e:LANES * h + FOX_HEAD * (e + 1)] = 1.0
        for x in range(3):
            pq[LANES * x + h, free + x] = 1.0
            pk[LANES * x + h, free + 3 + x] = -1.0
            oq[0, free + 3 + x] = 1.0
            ok[0, free + x] = 1.0
    return (jnp.asarray(real), (jnp.asarray(pq, BF16), jnp.asarray(oq)), (jnp.asarray(pk, BF16), jnp.asarray(ok)))


def _head_slots(x, real, bias):
    pairs = x.shape[1] // LANES
    dup = jnp.concatenate([x[:, LANES * p:LANES * (p + 1)] for p in range(pairs) for _ in range(2)], axis=1)
    return jnp.where(real > 0.0, dup, bias).astype(BF16)


def _bias_slots(c, p_ref, ones_ref):
    hi, mid, lo = _split3(c * LOG2E)
    return _dot(jnp.concatenate([hi, mid, lo], axis=1), p_ref[...]) + ones_ref[...]


def _shared_kv_kernel(x_ref, g_ref, wk_ref, wv_ref, wf_ref, bf_ref, kn_ref, bd_ref, tri_ref, *rest, aug):
    if aug:
        real_ref, pk_ref, ok_ref, k_ref, v_ref, lf_ref, c_ref, ka_ref, va_ref, carry = rest
    else:
        k_ref, v_ref, lf_ref, c_ref, carry = rest
    j = pl.program_id(1)
    heads = lf_ref.shape[1]

    @pl.when(j == 0)
    def _():
        carry[...] = jnp.zeros_like(carry)

    hn = _rms(x_ref[...], g_ref[...]).astype(BF16)
    k = _group_rms(_dot(hn, wk_ref[...]), bd_ref, FOX_HEAD) * kn_ref[...]
    v = _dot(hn, wv_ref[...])
    lf = _log_sigmoid(_dot(hn, wf_ref[...]) + bf_ref[...])
    c = _dot3_right(tri_ref[...], lf) + carry[...]
    carry[...] = c[c.shape[0] - 1:, :]
    lf_ref[...] = lf[:, :heads]
    c_ref[...] = c
    if aug:
        tm = k.shape[0]
        k_ref[0] = k.T.reshape(heads, FOX_HEAD, tm)
        v_ref[0] = v.T.reshape(heads, FOX_HEAD, tm)
        real = real_ref[...]
        ka_ref[...] = _head_slots(k, real, _bias_slots(c, pk_ref, ok_ref))
        va_ref[...] = _head_slots(v, real, 1.0)
    else:
        k_ref[...] = k
        v_ref[...] = v


def _shared_kv(x, g, wk, wv, wf, bf, kn, bd, tri, groups, heads, aug=None):
    n, d = x.shape
    tm = tri.shape[0]
    nj = n // groups // tm
    row = lambda w: pl.BlockSpec((tm, w), lambda b, j: (b * nj + j, 0))
    consts = [g, wk, wv, wf, bf, kn, bd, tri]
    tail_specs = [row(heads), row(LANES)]
    tail_shape = [jax.ShapeDtypeStruct((n, heads), F32), jax.ShapeDtypeStruct((n, LANES), F32)]
    if aug is None:
        extra = []
        out_specs = [row(d), row(d)] + tail_specs
        out_shape = [jax.ShapeDtypeStruct((n, d), F32)] * 2 + tail_shape
    else:
        real, _, (pk, ok) = aug
        extra = [real, pk, ok]
        kv_t = pl.BlockSpec((1, heads, FOX_HEAD, tm), lambda b, j: (b, 0, 0, j))
        out_specs = [kv_t, kv_t] + tail_specs + [row(heads * LANES)] * 2
        out_shape = ([jax.ShapeDtypeStruct((groups, heads, FOX_HEAD, nj * tm), F32)] * 2 + tail_shape
                     + [jax.ShapeDtypeStruct((n, heads * LANES), BF16)] * 2)
    return pl.pallas_call(
        functools.partial(_shared_kv_kernel, aug=aug is not None),
        grid=(groups, nj),
        in_specs=[row(d)] + [_const_spec(a) for a in consts + extra],
        out_specs=out_specs,
        out_shape=out_shape,
        scratch_shapes=[pltpu.VMEM((1, LANES), F32)],
        compiler_params=_params(("arbitrary", "arbitrary")),
    )(x, *consts, *extra)


def _fox_qg_kernel(x_ref, g_ref, w_ref, qn_ref, bd_ref, *rest, aug):
    d = x_ref.shape[1]
    qg = _dot(_rms(x_ref[...], g_ref[...]).astype(BF16), w_ref[...])
    q = _group_rms(qg[:, :d], bd_ref, FOX_HEAD) * (qn_ref[...] * FOX_HEAD ** -0.5)
    if aug:
        c_ref, real_ref, pq_ref, oq_ref, q_ref, gt_ref = rest
        q_ref[...] = _head_slots(q * LOG2E, real_ref[...], _bias_slots(c_ref[...], pq_ref, oq_ref))
    else:
        q_ref, gt_ref = rest
        q_ref[...] = q
    gt_ref[...] = qg[:, d:].astype(BF16)


def _fox_qg(x, g, w, qn, bd, c=None, aug=None):
    n, d = x.shape
    tm = min(ROW_TILE, n)
    row = lambda w_: pl.BlockSpec((tm, w_), lambda i: (i, 0))
    full = lambda a: pl.BlockSpec(a.shape, lambda i: (0,) * a.ndim)
    consts = [g, w, qn, bd]
    if aug is None:
        extra, extra_specs = [], []
        q_spec, q_shape = row(d), jax.ShapeDtypeStruct((n, d), F32)
    else:
        real, (pq, oq), _ = aug
        extra, extra_specs = [c, real, pq, oq], [row(LANES), full(real), full(pq), full(oq)]
        width = real.shape[1]
        q_spec, q_shape = row(width), jax.ShapeDtypeStruct((n, width), BF16)
    return pl.pallas_call(
        functools.partial(_fox_qg_kernel, aug=aug is not None),
        grid=(n // tm,),
        in_specs=[row(d)] + [_const_spec(a) for a in consts] + extra_specs,
        out_specs=[q_spec, row(d)],
        out_shape=[q_shape, jax.ShapeDtypeStruct((n, d), BF16)],
        compiler_params=_params(("arbitrary",)),
    )(x, *[_arr(a) for a in consts], *extra)


def _attn_prompt_kernel(q_ref, k_ref, v_ref, o_ref):
    i = pl.program_id(2)
    tq = q_ref.shape[0]

    def slot(ref, rows, e):
        return ref[rows, e * LANES:(e + 1) * LANES]

    def attend(tiles):
        def run():
            col = lax.broadcasted_iota(jnp.int32, (tq, tiles * tq), 1)
            row = lax.broadcasted_iota(jnp.int32, (tq, tiles * tq), 0)
            keep = col <= row + (tiles - 1) * tq
            keys = slice(0, tiles * tq)
            outs = []
            for e in range(2):
                s = jnp.where(keep, _dot_nt(slot(q_ref, slice(None), e), slot(k_ref, keys, e)), -jnp.inf)
                p = jnp.exp2(s - jnp.max(s, axis=-1, keepdims=True))
                acc = _dot(p.astype(BF16), slot(v_ref, keys, e))
                outs.append(acc / pltpu.roll(acc, FOX_HEAD, axis=1))
            lane = lax.broadcasted_iota(jnp.int32, (tq, LANES), 1)
            o_ref[...] = jnp.where(lane < FOX_HEAD, outs[0], outs[1]).astype(o_ref.dtype)
        return run

    for tiles in range(1, k_ref.shape[0] // tq + 1):
        pl.when(i == tiles - 1)(attend(tiles))


def _attn_prompt(qa, ka, va, batch):
    n, width = qa.shape
    t = n // batch
    tq = min(ATTN_TILE, t)
    nq = t // tq
    pairs = width // (2 * LANES)
    return pl.pallas_call(
        _attn_prompt_kernel,
        grid=(batch, pairs, nq),
        in_specs=[pl.BlockSpec((tq, 2 * LANES), lambda b, p, i: (b * nq + i, p)),
                  pl.BlockSpec((t, 2 * LANES), lambda b, p, i: (b, p)),
                  pl.BlockSpec((t, 2 * LANES), lambda b, p, i: (b, p))],
        out_specs=pl.BlockSpec((tq, LANES), lambda b, p, i: (b * nq + i, p)),
        out_shape=jax.ShapeDtypeStruct((n, pairs * LANES), BF16),
        compiler_params=_params(("arbitrary", "arbitrary", "arbitrary")),
    )(qa, ka, va)


def _page_prefix(n_pages, heads):
    r = np.arange(n_pages * heads)
    return jnp.asarray(((r[:, None] % heads) == (r[None, :] % heads)) & ((r[None, :] // heads) < (r[:, None] // heads)),
                       BF16)


def _attn_decode_kernel(pt_ref, q_ref, *refs, n_pages):
    del pt_ref
    kp, vp, lp = refs[:n_pages], refs[n_pages:2 * n_pages], refs[2 * n_pages:3 * n_pages]
    kn_ref, vn_ref, cn_ref, cq_ref, hm_ref, u_ref, px_ref, o_ref = refs[3 * n_pages:]
    seq, d = q_ref.shape[1:]
    hm = hm_ref[...]
    heads = hm.shape[0]
    rows = seq * heads
    q = (q_ref[0][:, None, :] * hm[None, :, :]).reshape(rows, d).astype(BF16)

    c_page = _dot3_left(jnp.concatenate([lp[g][0] for g in range(n_pages)], axis=0), u_ref[...])
    page_total = jnp.broadcast_to(c_page[:, PAGE - 1:PAGE], c_page.shape)
    before = _dot3_right(px_ref[...], page_total)
    c_past = c_page + before
    total = (before + page_total)[(n_pages - 1) * heads:, :]
    cq = cq_ref[0] + jnp.concatenate([total] * seq, axis=0)

    def update(state, s, ck, keep, pv):
        m, l, acc = state
        s = s + (jnp.concatenate([cq] * (s.shape[1] // PAGE), axis=1) - jnp.concatenate([ck] * seq, axis=0))
        if keep is not None:
            s = jnp.where(keep, s, -jnp.inf)
        m_new = jnp.maximum(m, jnp.max(s, axis=-1, keepdims=True))
        pr = jnp.exp(s - m_new)
        alpha = jnp.exp(m - m_new)
        return m_new, alpha * l + jnp.sum(pr, axis=-1, keepdims=True), alpha * acc + pv(pr.astype(BF16))

    state = (jnp.full((rows, 1), -jnp.inf, F32), jnp.zeros((rows, 1), F32), jnp.zeros((rows, d), F32))
    for g0 in range(0, n_pages, DECODE_PAGES):
        group = range(g0, g0 + DECODE_PAGES)
        k_cat = jnp.concatenate([kp[g][0].astype(BF16) for g in group], axis=1)
        v_cat = jnp.concatenate([vp[g][0].astype(BF16) for g in group], axis=1)
        ck = jnp.concatenate([c_past[g * heads:(g + 1) * heads, :] for g in group], axis=1)
        state = update(state, _dot(q, k_cat), ck, None, lambda pr, v_cat=v_cat: _dot_nt(pr, v_cat))

    pad = jnp.zeros((PAGE - seq, d), F32)
    k_new = jnp.concatenate([kn_ref[0], pad], axis=0).astype(BF16)
    v_new = jnp.concatenate([vn_ref[0], pad], axis=0).astype(BF16)
    t = lax.broadcasted_iota(jnp.int32, (rows, PAGE), 0) // heads
    key = lax.broadcasted_iota(jnp.int32, (rows, PAGE), 1)
    _, l, acc = update(state, _dot_nt(q, k_new), cn_ref[0] + total, key <= t, lambda pr: _dot(pr, v_new))
    o_ref[0] = jnp.sum((acc / l).reshape(seq, heads, d) * hm[None, :, :], axis=1)


def _attn_decode(q, cache_kt, cache_vt, cache_lt, k_new, v_new, c_new_t, cq_col, head_mask, upper, prefix,
                 page_table):
    bd, seq, d = q.shape
    n_pages = page_table.shape[1]
    heads = head_mask.shape[0]
    page = lambda g, w: pl.BlockSpec((1, w, PAGE), lambda s, pt: (pt[s * n_pages + g], 0, 0))
    tok = pl.BlockSpec((1, seq, d), lambda s, pt: (s, 0, 0))
    grid_spec = pltpu.PrefetchScalarGridSpec(
        num_scalar_prefetch=1,
        grid=(bd,),
        in_specs=[tok] + [page(g, d) for g in range(n_pages)] + [page(g, d) for g in range(n_pages)]
        + [page(g, heads) for g in range(n_pages)] + [
            tok, tok,
            pl.BlockSpec((1, heads, PAGE), lambda s, pt: (s, 0, 0)),
            pl.BlockSpec((1, seq * heads, 1), lambda s, pt: (s, 0, 0)),
            _const_spec(head_mask), _const_spec(upper), _const_spec(prefix)],
        out_specs=tok,
    )
    return pl.pallas_call(
        functools.partial(_attn_decode_kernel, n_pages=n_pages),
        grid_spec=grid_spec,
        out_shape=jax.ShapeDtypeStruct((bd, seq, d), F32),
        compiler_params=_params(("arbitrary",)),
    )(page_table.reshape(-1), q, *([cache_kt] * n_pages), *([cache_vt] * n_pages), *([cache_lt] * n_pages),
      k_new, v_new, c_new_t, cq_col, head_mask, upper, prefix)


def _trunk(x, seq, hgrn_s0, conv_s0, past, w):
    n, d = x.shape
    n_seq = n // seq
    depth = w["ffn_w_in"].shape[0]
    n_a = w["hgrn_w_in"].shape[0]
    heads = w["fox_b_f"].shape[0]
    prompt = past is None
    h = x
    new_hgrn, new_conv = [], []
    for layer in range(depth):
        if layer < n_a:
            q, k, lf, i, gt = _hgrn_in(h, (w["norm_mix"], layer), (w["hgrn_w_in"], layer), w["hgrn_lb_param"], layer)
            if prompt:
                o, s = _hgrn_prompt(q, k, lf, i, n_seq, w["hgrn_prompt_consts"])
            else:
                o, s = _hgrn_decode(q, k, lf, i, hgrn_s0, layer, seq, w["hgrn_decode_consts"])
            new_hgrn.append(s)
            mix = ("hgrn", o, gt, h, [(w["hgrn_onorm"], layer), (w["hgrn_w_o"], layer)])
        else:
            j = layer - n_a
            qg = (h, (w["norm_mix"], layer), (w["fox_w_qg"], j), (w["q_norm"], j), w["bd_fox"])
            if prompt:
                qa, gt = _fox_qg(*qg, c, w["aug"])
                o = _attn_prompt(qa, ka, va, n_seq)
            else:
                q, gt = _fox_qg(*qg)
                o = _attn_decode(q.reshape(n_seq, seq, d), *past[:3], k_sh.reshape(n_seq, seq, d),
                                 v_sh.reshape(n_seq, seq, d), c_new_t, cq_col, w["head_mask"], w["upper"],
                                 w["page_prefix"], past[3]).reshape(n, d)
            mix = ("fox", o, gt, h, [(w["fox_w_o"], j)])
        ffn = [(w[name], layer) for name in ("norm_ffn", "ffn_w_in", "ffn_conv_w", "ffn_conv_b", "ffn_w_out")]
        if prompt:
            h, cs = _ffn_prompt(*mix, ffn, n_seq)
        else:
            h, cs = _ffn_decode(*mix, ffn, conv_s0, layer, seq)
        new_conv.append(cs)
        if layer == n_a - 1:
            tri = w["tri_prompt"] if prompt else w["tri_decode"]
            kv = _shared_kv(h, w["kv_norm"], w["w_k"], w["w_v"], w["w_f"], w["b_f"], w["k_norm"], w["bd_fox"], tri,
                            n_seq if prompt else n // tri.shape[0], heads, w["aug"] if prompt else None)
            k_sh, v_sh, logf, c = kv[:4]
            if prompt:
                ka, va = kv[4:]
            else:
                c3 = c[:, :heads].reshape(n_seq, seq, heads)
                cq_col = c3.reshape(n_seq, seq * heads, 1)
                c_new_t = jnp.pad(c3.transpose(0, 2, 1), ((0, 0), (0, 0), (0, PAGE - seq)))
    return h, k_sh, v_sh, logf, jnp.stack(new_hgrn), jnp.stack(new_conv)


def kernel(x_prompt, x_sample, cache_k, cache_v, cache_logf, state_hgrn, state_conv, page_table, norm_mix, norm_ffn, hgrn_w_in, hgrn_lb_param, hgrn_onorm, hgrn_w_o, kv_norm, w_kvf, fox_b_f, k_norm, fox_w_qg, q_norm, fox_w_o, ffn_w_in, ffn_conv_w, ffn_conv_b, ffn_w_out):
    b, t, d = x_prompt.shape
    bd, ts, _ = x_sample.shape
    heads = fox_b_f.shape[0]
    n_phys, page = cache_k.shape[:2]
    assert page == PAGE and d % LANES == 0 and d // heads == FOX_HEAD
    assert t % ATTN_TILE == 0 and page_table.shape[1] % DECODE_PAGES == 0
    assert t % ROW_TILE == 0 and (bd * ts) % ROW_TILE == 0 and CHUNK % ts == 0 and ROW_TILE % ts == 0
    assert ts >= CONV_W - 1

    row = lambda a: a.reshape(a.shape[:-1] + (1, a.shape[-1]))
    w = {
        "norm_mix": row(norm_mix), "norm_ffn": row(norm_ffn),
        "hgrn_w_in": hgrn_w_in.astype(BF16), "hgrn_lb_param": hgrn_lb_param,
        "hgrn_onorm": row(jnp.tile(hgrn_onorm, (1, d // HGRN_HEAD))), "hgrn_w_o": hgrn_w_o.astype(BF16),
        "kv_norm": row(kv_norm),
        "w_k": w_kvf[:, :d].astype(BF16), "w_v": w_kvf[:, d:2 * d].astype(BF16),
        "w_f": jnp.pad(w_kvf[:, 2 * d:], ((0, 0), (0, LANES - heads))).astype(BF16),
        "b_f": jnp.pad(fox_b_f, (0, LANES - heads)).reshape(1, LANES), "fox_b_f": fox_b_f,
        "k_norm": jnp.tile(k_norm, heads).reshape(1, d),
        "fox_w_qg": fox_w_qg.astype(BF16), "q_norm": row(jnp.tile(q_norm, (1, heads))),
        "fox_w_o": fox_w_o.astype(BF16),
        "ffn_w_in": ffn_w_in.astype(BF16), "ffn_conv_w": ffn_conv_w, "ffn_conv_b": row(ffn_conv_b),
        "ffn_w_out": ffn_w_out.astype(BF16),
        "hgrn_prompt_consts": _level_mats(CHUNK, CHUNK), "hgrn_decode_consts": _level_mats(CHUNK, ts),
        "bd_fox": _block_diag_ones(d, FOX_HEAD),
        "tri_prompt": _seg_lower_tri(ROW_TILE, ROW_TILE), "tri_decode": _seg_lower_tri(ROW_TILE, ts),
        "upper": _upper_tri(PAGE), "head_mask": _head_mask(heads, FOX_HEAD),
        "aug": _aug_consts(heads), "page_prefix": _page_prefix(page_table.shape[1], heads),
    }

    y_p, k_p, v_p, lf_p, hgrn_p, conv_p = _trunk(x_prompt.reshape(b * t, d), t, None, None, None, w)

    past = (cache_k.transpose(0, 2, 3, 1).reshape(n_phys, d, page), cache_v.transpose(0, 2, 3, 1).reshape(n_phys, d, page),
            cache_logf.astype(F32).transpose(0, 2, 1), page_table)
    y_s, k_s, v_s, lf_s, hgrn_s, conv_s = _trunk(x_sample.reshape(bd * ts, d), ts, state_hgrn, state_conv, past, w)

    hd = (heads, d // heads)
    return (y_p.reshape(b, t, d), y_s.reshape(bd, ts, d),
            k_p.transpose(0, 3, 1, 2), v_p.transpose(0, 3, 1, 2), lf_p.reshape(b, t, heads),
            k_s.reshape((bd, ts) + hd), v_s.reshape((bd, ts) + hd), lf_s.reshape(bd, ts, heads),
            hgrn_p, hgrn_s, conv_p, conv_s)
```
